```python
import jax, jax.numpy as jnp
from jax import lax
import numpy as np

D_MODEL = 1024
BATCH = 8
SEQ = 4096
DEPTH = 2
DEC_BATCH = 16
DEC_SEQ = 16
PAST_LEN = 2048

CHUNK = 64
MIX_WIDTH = D_MODEL
RET_WIDTH = MIX_WIDTH // 2
CONV_WIDTH = MIX_WIDTH - RET_WIDTH
N_RET_HEADS = 4
RET_HEAD_DIM = RET_WIDTH // N_RET_HEADS
CONV_K = 3
D_FF = 4 * D_MODEL
ROPE_BASE = 10000.0
RMS_EPS = 1e-6
GN_EPS = 1e-5
IN_COLS = 4 * RET_WIDTH + 3 * CONV_WIDTH
SPLITS = [RET_WIDTH, 2 * RET_WIDTH, 3 * RET_WIDTH, 4 * RET_WIDTH,
          4 * RET_WIDTH + CONV_WIDTH, 4 * RET_WIDTH + 2 * CONV_WIDTH]

kernel_name = "hymba_retention_shortconv_stream_step"


def rms_norm(x, w):
    xf = x.astype(jnp.float32)
    y = xf * lax.rsqrt(jnp.mean(xf * xf, axis=-1, keepdims=True) + RMS_EPS)
    return (y * w.astype(jnp.float32)).astype(x.dtype)


def retention_log_decay():
    return jnp.log(1.0 - jnp.exp2(-5.0 - jnp.arange(N_RET_HEADS, dtype=jnp.float32)))


def rope(x, pos):
    half = RET_HEAD_DIM // 2
    inv = ROPE_BASE ** (-jnp.arange(half, dtype=jnp.float32) / half)
    ang = pos.astype(jnp.float32)[:, None] * inv[None, :]
    cos = jnp.cos(ang)[None, :, None, :]
    sin = jnp.sin(ang)[None, :, None, :]
    x1, x2 = x[..., :half], x[..., half:]
    return jnp.concatenate([x1 * cos - x2 * sin, x1 * sin + x2 * cos], axis=-1)


def retention_chunkwise(q, k, v, r0, chunk):
    b, l, h, d = q.shape
    nc = l // chunk

    def to_chunks(t):
        return t.reshape(b, nc, chunk, h, d).transpose(1, 0, 3, 2, 4)

    lg = retention_log_decay()
    idx = jnp.arange(chunk, dtype=jnp.float32)
    diff = idx[:, None] - idx[None, :]
    causal = diff >= 0
    dmat = jnp.where(causal[None], jnp.exp(jnp.where(causal, diff, 0.0)[None] * lg[:, None, None]), 0.0)
    q_dec = jnp.exp((idx[None, :] + 1.0) * lg[:, None])
    k_dec = jnp.exp((chunk - 1.0 - idx[None, :]) * lg[:, None])
    chunk_dec = jnp.exp(chunk * lg)

    def step(r, qkv):
        qc, kc, vc = qkv
        s = jnp.einsum('bhnd,bhmd->bhnm', qc, kc) * dmat[None]
        o = (jnp.einsum('bhnm,bhmv->bhnv', s, vc)
             + jnp.einsum('bhnd,bhdv->bhnv', qc * q_dec[None, :, :, None], r))
        r = (r * chunk_dec[None, :, None, None]
             + jnp.einsum('bhmd,bhmv->bhdv', kc * k_dec[None, :, :, None], vc))
        return r, o

    r_final, o = lax.scan(step, r0, (to_chunks(q), to_chunks(k), to_chunks(v)))
    o = o.transpose(1, 0, 3, 2, 4).reshape(b, l, h, d)
    return o, r_final


def hybrid_layer(x, pos, r0, conv_buf, chunk, ln1_w, w_in, conv_w, ret_norm_w, w_out,
                 ln2_w, w_up, w_down):
    b, l, _ = x.shape
    n = rms_norm(x, ln1_w)
    proj = n @ w_in
    q, k, v, g, gate_b, gate_c, hc = jnp.split(proj, SPLITS, axis=-1)

    shp = (b, l, N_RET_HEADS, RET_HEAD_DIM)
    qf = rope(q.reshape(shp).astype(jnp.float32), pos)
    kf = rope(k.reshape(shp).astype(jnp.float32), pos) * (RET_HEAD_DIM ** -0.5)
    vf = v.reshape(shp).astype(jnp.float32)
    o, r_new = retention_chunkwise(qf, kf, vf, r0.astype(jnp.float32), chunk)
    mu = jnp.mean(o, axis=-1, keepdims=True)
    oc = o - mu
    o = oc * lax.rsqrt(jnp.mean(oc * oc, axis=-1, keepdims=True) + GN_EPS)
    o = o.reshape(b, l, RET_WIDTH) * ret_norm_w.astype(jnp.float32)
    ret_out = (jax.nn.silu(g.astype(jnp.float32)) * o).astype(x.dtype)

    u = gate_c * hc
    padded = jnp.concatenate([conv_buf.astype(u.dtype), u], axis=1)
    z = padded[:, 0:l] * conv_w[:, 0]
    for j in range(1, CONV_K):
        z = z + padded[:, j:j + l] * conv_w[:, j]
    conv_out = gate_b * z
    new_buf = padded[:, l:]

    x = x + jnp.concatenate([ret_out, conv_out], axis=-1) @ w_out

    h2 = rms_norm(x, ln2_w)
    x = x + jnp.square(jax.nn.relu(h2 @ w_up)) @ w_down
    return x, r_new, new_buf


def setup_inputs(seed: int = 0) -> dict:
    key = jax.random.key(seed)
    ks = jax.random.split(key, 13)
    f32 = jnp.float32
    nrm = jax.random.normal
    return {
        "x_prompt": nrm(ks[0], (BATCH, SEQ, D_MODEL), f32),
        "x_sample": nrm(ks[1], (DEC_BATCH, DEC_SEQ, D_MODEL), f32),
        "state_ret": 0.1 * nrm(ks[2], (DEPTH, DEC_BATCH, N_RET_HEADS, RET_HEAD_DIM, RET_HEAD_DIM), f32),
        "state_conv": nrm(ks[3], (DEPTH, DEC_BATCH, CONV_K - 1, CONV_WIDTH), f32),
        "ln1_w": 1.0 + 0.02 * nrm(ks[4], (DEPTH, D_MODEL), f32),
        "w_in": nrm(ks[5], (DEPTH, D_MODEL, IN_COLS), f32) * D_MODEL ** -0.5,
        "conv_w": nrm(ks[6], (DEPTH, CONV_WIDTH, CONV_K), f32) * CONV_K ** -0.5,
        "ret_norm_w": 1.0 + 0.02 * nrm(ks[7], (DEPTH, RET_WIDTH), f32),
        "w_out": nrm(ks[8], (DEPTH, MIX_WIDTH, D_MODEL), f32) * MIX_WIDTH ** -0.5,
        "ln2_w": 1.0 + 0.02 * nrm(ks[9], (DEPTH, D_MODEL), f32),
        "w_mlp_up": nrm(ks[10], (DEPTH, D_MODEL, D_FF), f32) * D_MODEL ** -0.5,
        "w_mlp_down": nrm(ks[11], (DEPTH, D_FF, D_MODEL), f32) * D_FF ** -0.5,
        "ln_f_w": 1.0 + 0.02 * nrm(ks[12], (D_MODEL,), f32),
    }


def reference(x_prompt, x_sample, state_ret, state_conv, ln1_w, w_in, conv_w, ret_norm_w,
              w_out, ln2_w, w_mlp_up, w_mlp_down, ln_f_w):
    b_p, l_p, _ = x_prompt.shape
    b_s, l_s, _ = x_sample.shape
    pos_p = jnp.arange(l_p)
    pos_s = PAST_LEN + jnp.arange(l_s)
    chunk_p = min(CHUNK, l_p)
    chunk_s = l_s

    xp, xs = x_prompt, x_sample
    ret_p, conv_p, ret_s, conv_s = [], [], [], []
    for i in range(DEPTH):
        r0_p = jnp.zeros((b_p, N_RET_HEADS, RET_HEAD_DIM, RET_HEAD_DIM), jnp.float32)
        buf0_p = jnp.zeros((b_p, CONV_K - 1, CONV_WIDTH), x_prompt.dtype)
        xp, rp, bp = hybrid_layer(xp, pos_p, r0_p, buf0_p, chunk_p, ln1_w[i], w_in[i], conv_w[i],
                                  ret_norm_w[i], w_out[i], ln2_w[i], w_mlp_up[i], w_mlp_down[i])
        xs, rs, bs = hybrid_layer(xs, pos_s, state_ret[i], state_conv[i], chunk_s, ln1_w[i], w_in[i],
                                  conv_w[i], ret_norm_w[i], w_out[i], ln2_w[i], w_mlp_up[i],
                                  w_mlp_down[i])
        ret_p.append(rp)
        conv_p.append(bp)
        ret_s.append(rs)
        conv_s.append(bs)

    y_prompt = rms_norm(xp, ln_f_w)
    y_sample = rms_norm(xs, ln_f_w)
    new_ret_prompt = jnp.stack(ret_p).astype(state_ret.dtype)
    new_conv_prompt = jnp.stack(conv_p).astype(state_conv.dtype)
    new_ret_sample = jnp.stack(ret_s).astype(state_ret.dtype)
    new_conv_sample = jnp.stack(conv_s).astype(state_conv.dtype)
    return (y_prompt, y_sample, new_ret_prompt, new_conv_prompt, new_ret_sample, new_conv_sample)
```

```python
import functools

import numpy as np
import jax
import jax.numpy as jnp
from jax import lax
from jax.experimental import pallas as pl
from jax.experimental.pallas import tpu as pltpu

D_MODEL = 1024
N_HEADS = 4
HEAD_DIM = 128
RET_WIDTH = N_HEADS * HEAD_DIM
CONV_WIDTH = D_MODEL - RET_WIDTH
CONV_K = 3
D_FF = 4 * D_MODEL
IN_COLS = 4 * RET_WIDTH + 3 * CONV_WIDTH
PAST_LEN = 2048
ROPE_BASE = 10000.0
RMS_EPS = 1e-6
GN_EPS = 1e-5

V7X_VMEM_LIMIT_BYTES = 56 * 1024 * 1024
F32_SUBLANES = 8

BF16 = jnp.bfloat16
F32 = jnp.float32


def _rope_tables(pos0, length):
    half = HEAD_DIM // 2
    inv = ROPE_BASE ** (-np.arange(half, dtype=np.float64) / half)
    ang = (pos0 + np.arange(length, dtype=np.float64))[:, None] * inv[None, :]
    cos, sin = np.cos(ang), np.sin(ang)
    cos2 = np.concatenate([cos, cos], axis=-1)
    sin2 = np.concatenate([-sin, sin], axis=-1)
    return jnp.asarray(cos2, F32), jnp.asarray(sin2, F32)


def _decay_tables(chunk):
    lg = np.log(1.0 - np.exp2(-5.0 - np.arange(N_HEADS, dtype=np.float64)))
    idx = np.arange(chunk, dtype=np.float64)
    diff = idx[:, None] - idx[None, :]
    scale = HEAD_DIM ** -0.5
    dmat = np.where(diff >= 0, np.exp(np.maximum(diff, 0.0)[None] * lg[:, None, None]), 0.0) * scale
    qdec = np.exp((idx[None, :] + 1.0) * lg[:, None])
    kdec = np.exp((chunk - 1.0 - idx[None, :]) * lg[:, None]) * scale
    cdec = np.exp(chunk * lg)
    lanes = np.ones((1, 1, HEAD_DIM))
    return (jnp.asarray(dmat, F32),
            jnp.asarray(qdec[:, :, None] * lanes, F32),
            jnp.asarray(kdec[:, :, None] * lanes, F32),
            jnp.asarray(cdec[:, None, None] * np.ones((1, F32_SUBLANES, HEAD_DIM)), F32))


def _rms_norm(x, w):
    return x * lax.rsqrt(jnp.mean(x * x, axis=-1, keepdims=True) + RMS_EPS) * w


def _mixer_kernel(x_ref, ln1_ref, win_ref, cos_ref, sin_ref, dmat_ref, qdec_ref, kdec_ref,
                  cdec_ref, convw_ref, retw_ref, wout_ref, r0_ref, cb0_ref,
                  xo_ref, rnew_ref, cnew_ref, mix_ref, ubuf_ref, *, nb, tm, chunk):
    li = pl.program_id(1)
    rows = nb * tm

    @pl.when(li == 0)
    def _():
        rnew_ref[...] = r0_ref[...]
        cnew_ref[...] = cb0_ref[...]

    x = x_ref[...].reshape(rows, D_MODEL)
    n = _rms_norm(x, ln1_ref[...]).astype(BF16)
    proj = jnp.dot(n, win_ref[...], preferred_element_type=F32)

    def rope(t, cos, sin):
        return t * cos + pltpu.roll(t, HEAD_DIM // 2, axis=1) * sin

    for s in range(nb):
        for c in range(tm // chunk):
            r0 = s * tm + c * chunk
            cos = cos_ref[c * chunk:(c + 1) * chunk, :]
            sin = sin_ref[c * chunk:(c + 1) * chunk, :]
            for h in range(N_HEADS):
                lo = h * HEAD_DIM
                q = rope(proj[r0:r0 + chunk, lo:lo + HEAD_DIM], cos, sin)
                k = rope(proj[r0:r0 + chunk, RET_WIDTH + lo:RET_WIDTH + lo + HEAD_DIM], cos, sin)
                v = proj[r0:r0 + chunk, 2 * RET_WIDTH + lo:2 * RET_WIDTH + lo + HEAD_DIM].astype(BF16)
                g = proj[r0:r0 + chunk, 3 * RET_WIDTH + lo:3 * RET_WIDTH + lo + HEAD_DIM]
                state = rnew_ref[s, h]
                sc = lax.dot_general(q.astype(BF16), k.astype(BF16), (((1,), (1,)), ((), ())),
                                     preferred_element_type=F32) * dmat_ref[h]
                o = (jnp.dot(sc.astype(BF16), v, preferred_element_type=F32)
                     + jnp.dot((q * qdec_ref[h]).astype(BF16), state.astype(BF16),
                               preferred_element_type=F32))
                kd = (k * kdec_ref[h]).astype(BF16)
                rnew_ref[s, h] = (state * cdec_ref[h, 0:1, :]
                                  + lax.dot_general(kd, v, (((0,), (0,)), ((), ())),
                                                    preferred_element_type=F32))
                oc = o - jnp.mean(o, axis=-1, keepdims=True)
                on = oc * lax.rsqrt(jnp.mean(oc * oc, axis=-1, keepdims=True) + GN_EPS)
                ret = g * jax.nn.sigmoid(g) * (on * retw_ref[:, lo:lo + HEAD_DIM])
                mix_ref[r0:r0 + chunk, lo:lo + HEAD_DIM] = ret.astype(BF16)

        base = 4 * RET_WIDTH
        gate_b = proj[s * tm:(s + 1) * tm, base:base + CONV_WIDTH]
        gate_c = proj[s * tm:(s + 1) * tm, base + CONV_WIDTH:base + 2 * CONV_WIDTH]
        hc = proj[s * tm:(s + 1) * tm, base + 2 * CONV_WIDTH:base + 3 * CONV_WIDTH]
        u = gate_c * hc
        pad = F32_SUBLANES
        ubuf_ref[pad - (CONV_K - 1):pad, :] = cnew_ref[s]
        ubuf_ref[pad:pad + tm, :] = u
        z = (ubuf_ref[pad - 2:pad - 2 + tm, :] * convw_ref[0:1, :]
             + ubuf_ref[pad - 1:pad - 1 + tm, :] * convw_ref[1:2, :]
             + u * convw_ref[2:3, :])
        cnew_ref[s] = ubuf_ref[pad + tm - (CONV_K - 1):pad + tm, :]
        mix_ref[s * tm:(s + 1) * tm, RET_WIDTH:] = (gate_b * z).astype(BF16)

    y = x + jnp.dot(mix_ref[...], wout_ref[...], preferred_element_type=F32)
    xo_ref[...] = y.reshape(nb, tm, D_MODEL)


def _mixer_call(x, ln1, w_in, conv_w, ret_w, w_out, r0, cb0, *, pos0, nb, tm, chunk):
    b, l, _ = x.shape
    assert b % nb == 0 and l % tm == 0 and tm % chunk == 0 and chunk % 16 == 0
    cos2, sin2 = _rope_tables(pos0, l)
    dmat, qdec, kdec, cdec = _decay_tables(chunk)
    rows = nb * tm
    const = lambda shape: pl.BlockSpec(shape, lambda bi, li: (0,) * len(shape))
    kernel = functools.partial(_mixer_kernel, nb=nb, tm=tm, chunk=chunk)
    return pl.pallas_call(
        kernel,
        grid=(b // nb, l // tm),
        in_specs=[
            pl.BlockSpec((nb, tm, D_MODEL), lambda bi, li: (bi, li, 0)),
            const((1, D_MODEL)),
            const((D_MODEL, IN_COLS)),
            pl.BlockSpec((tm, HEAD_DIM), lambda bi, li: (li, 0)),
            pl.BlockSpec((tm, HEAD_DIM), lambda bi, li: (li, 0)),
            const((N_HEADS, chunk, chunk)),
            const((N_HEADS, chunk, HEAD_DIM)),
            const((N_HEADS, chunk, HEAD_DIM)),
            const((N_HEADS, F32_SUBLANES, HEAD_DIM)),
            const((CONV_K, CONV_WIDTH)),
            const((1, RET_WIDTH)),
            const((D_MODEL, D_MODEL)),
            pl.BlockSpec((nb, N_HEADS, HEAD_DIM, HEAD_DIM), lambda bi, li: (bi, 0, 0, 0)),
            pl.BlockSpec((nb, CONV_K - 1, CONV_WIDTH), lambda bi, li: (bi, 0, 0)),
        ],
        out_specs=[
            pl.BlockSpec((nb, tm, D_MODEL), lambda bi, li: (bi, li, 0)),
            pl.BlockSpec((nb, N_HEADS, HEAD_DIM, HEAD_DIM), lambda bi, li: (bi, 0, 0, 0)),
            pl.BlockSpec((nb, CONV_K - 1, CONV_WIDTH), lambda bi, li: (bi, 0, 0)),
        ],
        out_shape=[
            jax.ShapeDtypeStruct((b, l, D_MODEL), F32),
            jax.ShapeDtypeStruct((b, N_HEADS, HEAD_DIM, HEAD_DIM), F32),
            jax.ShapeDtypeStruct((b, CONV_K - 1, CONV_WIDTH), F32),
        ],
        scratch_shapes=[
            pltpu.VMEM((rows, D_MODEL), BF16),
            pltpu.VMEM((F32_SUBLANES + tm, CONV_WIDTH), F32),
        ],
        compiler_params=pltpu.CompilerParams(
            dimension_semantics=("arbitrary", "arbitrary"),
            vmem_limit_bytes=V7X_VMEM_LIMIT_BYTES),
        name="mixer",
    )(x, ln1, w_in, cos2, sin2, dmat, qdec, kdec, cdec, conv_w, ret_w, w_out, r0, cb0)


def _mlp_kernel(x_ref, ln2_ref, wup_ref, wdown_ref, lnf_ref, o_ref, *, final_norm):
    x = x_ref[...]
    h2 = _rms_norm(x, ln2_ref[...]).astype(BF16)
    up = jnp.dot(h2, wup_ref[...], preferred_element_type=F32)
    act = jnp.square(jnp.maximum(up, 0.0)).astype(BF16)
    y = x + jnp.dot(act, wdown_ref[...], preferred_element_type=F32)
    if final_norm:
        y = _rms_norm(y, lnf_ref[...])
    o_ref[...] = y


def _mlp_call(x2d, ln2, w_up, w_down, ln_f, *, tm, final_norm):
    t, _ = x2d.shape
    assert t % tm == 0
    const = lambda shape: pl.BlockSpec(shape, lambda i: (0,) * len(shape))
    return pl.pallas_call(
        functools.partial(_mlp_kernel, final_norm=final_norm),
        grid=(t // tm,),
        in_specs=[
            pl.BlockSpec((tm, D_MODEL), lambda i: (i, 0)),
            const((1, D_MODEL)),
            const((D_MODEL, D_FF)),
            const((D_FF, D_MODEL)),
            const((1, D_MODEL)),
        ],
        out_specs=pl.BlockSpec((tm, D_MODEL), lambda i: (i, 0)),
        out_shape=jax.ShapeDtypeStruct((t, D_MODEL), F32),
        compiler_params=pltpu.CompilerParams(
            dimension_semantics=("arbitrary",),
            vmem_limit_bytes=V7X_VMEM_LIMIT_BYTES),
        name="mlp",
    )(x2d, ln2, w_up, w_down, ln_f)


def kernel(x_prompt, x_sample, state_ret, state_conv, ln1_w, w_in, conv_w, ret_norm_w, w_out, ln2_w, w_mlp_up, w_mlp_down, ln_f_w):
    depth = w_in.shape[0]
    b_p, l_p, _ = x_prompt.shape
    b_s, l_s, _ = x_sample.shape

    w_in_b = w_in.astype(BF16)
    w_out_b = w_out.astype(BF16)
    w_up_b = w_mlp_up.astype(BF16)
    w_down_b = w_mlp_down.astype(BF16)
    conv_w_t = jnp.swapaxes(conv_w, 1, 2)
    ln_f = ln_f_w.reshape(1, D_MODEL)

    r0_p = jnp.zeros((b_p, N_HEADS, HEAD_DIM, HEAD_DIM), F32)
    cb0_p = jnp.zeros((b_p, CONV_K - 1, CONV_WIDTH), F32)

    xp, xs = x_prompt, x_sample
    ret_p, conv_p, ret_s, conv_s = [], [], [], []
    for i in range(depth):
        last = i == depth - 1
        layer = dict(ln1=ln1_w[i].reshape(1, D_MODEL), w_in=w_in_b[i], conv_w=conv_w_t[i],
                     ret_w=ret_norm_w[i].reshape(1, RET_WIDTH), w_out=w_out_b[i])
        xp, rp, cp = _mixer_call(xp, r0=r0_p, cb0=cb0_p, pos0=0, nb=1, tm=512, chunk=256, **layer)
        xs, rs, cs = _mixer_call(xs, r0=state_ret[i], cb0=state_conv[i], pos0=PAST_LEN,
                                 nb=b_s, tm=l_s, chunk=l_s, **layer)
        mlp = dict(ln2=ln2_w[i].reshape(1, D_MODEL), w_up=w_up_b[i], w_down=w_down_b[i], ln_f=ln_f,
                   final_norm=last)
        xp = _mlp_call(xp.reshape(b_p * l_p, D_MODEL), tm=512, **mlp).reshape(b_p, l_p, D_MODEL)
        xs = _mlp_call(xs.reshape(b_s * l_s, D_MODEL), tm=b_s * l_s, **mlp).reshape(b_s, l_s, D_MODEL)
        ret_p.append(rp)
        conv_p.append(cp)
        ret_s.append(rs)
        conv_s.append(cs)

    return (xp, xs, jnp.stack(ret_p), jnp.stack(conv_p), jnp.stack(ret_s), jnp.stack(conv_s))
```

```python
import functools

import numpy as np
import jax
import jax.numpy as jnp
from jax import lax
from jax.experimental import pallas as pl
from jax.experimental.pallas import tpu as pltpu

D_MODEL = 1024
N_HEADS = 4
HEAD_DIM = 128
RET_WIDTH = N_HEADS * HEAD_DIM
CONV_WIDTH = D_MODEL - RET_WIDTH
CONV_K = 3
D_FF = 4 * D_MODEL
IN_COLS = 4 * RET_WIDTH + 3 * CONV_WIDTH
PAST_LEN = 2048
ROPE_BASE = 10000.0
RMS_EPS = 1e-6
GN_EPS = 1e-5

V7X_VMEM_LIMIT_BYTES = 60 * 1024 * 1024
F32_SUBLANES = 8
PROMPT_TILE = 512
PROMPT_CHUNK = 256

BF16 = jnp.bfloat16
F32 = jnp.float32


def _rope_tables(pos0, length):
    half = HEAD_DIM // 2
    inv = ROPE_BASE ** (-np.arange(half, dtype=np.float64) / half)
    ang = (pos0 + np.arange(length, dtype=np.float64))[:, None] * inv[None, :]
    cos, sin = np.cos(ang), np.sin(ang)
    cos2 = np.concatenate([cos, cos], axis=-1)
    sin2 = np.concatenate([-sin, sin], axis=-1)
    return jnp.asarray(cos2, F32), jnp.asarray(sin2, F32)


def _decay_tables(chunk):
    lg = np.log(1.0 - np.exp2(-5.0 - np.arange(N_HEADS, dtype=np.float64)))
    idx = np.arange(chunk, dtype=np.float64)
    diff = idx[:, None] - idx[None, :]
    scale = HEAD_DIM ** -0.5
    dmat = np.where(diff >= 0, np.exp(np.maximum(diff, 0.0)[None] * lg[:, None, None]), 0.0) * scale
    qdec = np.exp((idx[None, :] + 1.0) * lg[:, None])
    kdec = np.exp((chunk - 1.0 - idx[None, :]) * lg[:, None]) * scale
    cdec = np.exp(chunk * lg)
    lanes = np.ones((1, 1, HEAD_DIM))
    return (jnp.asarray(dmat, F32),
            jnp.asarray(qdec[:, :, None] * lanes, F32),
            jnp.asarray(kdec[:, :, None] * lanes, F32),
            jnp.asarray(cdec[:, None, None] * np.ones((1, F32_SUBLANES, HEAD_DIM)), F32))


def _rms_norm(x, w):
    return x * lax.rsqrt(jnp.mean(x * x, axis=-1, keepdims=True) + RMS_EPS) * w


def _mixer_tile(load_x, ln1_ref, win_ref, cos_ref, sin_ref, dmat_ref, qdec_ref, kdec_ref, cdec_ref,
                convw_ref, retw_ref, wout_ref, rstate_ref, cstate_ref, mix_ref, ubuf_ref,
                *, nb, tm, chunk):
    n = _rms_norm(load_x(), ln1_ref[...]).astype(BF16)
    proj = jnp.dot(n, win_ref[...], preferred_element_type=F32)

    def rope(t, cos, sin):
        return t * cos + pltpu.roll(t, HEAD_DIM // 2, axis=1) * sin

    for s in range(nb):
        for c in range(tm // chunk):
            r0 = s * tm + c * chunk
            cos = cos_ref[c * chunk:(c + 1) * chunk, :]
            sin = sin_ref[c * chunk:(c + 1) * chunk, :]
            for h in range(N_HEADS):
                lo = h * HEAD_DIM
                q = rope(proj[r0:r0 + chunk, lo:lo + HEAD_DIM], cos, sin)
                k = rope(proj[r0:r0 + chunk, RET_WIDTH + lo:RET_WIDTH + lo + HEAD_DIM], cos, sin)
                v = proj[r0:r0 + chunk, 2 * RET_WIDTH + lo:2 * RET_WIDTH + lo + HEAD_DIM].astype(BF16)
                g = proj[r0:r0 + chunk, 3 * RET_WIDTH + lo:3 * RET_WIDTH + lo + HEAD_DIM]
                state = rstate_ref[s, h]
                sc = lax.dot_general(q.astype(BF16), k.astype(BF16), (((1,), (1,)), ((), ())),
                                     preferred_element_type=F32) * dmat_ref[h]
                o = (jnp.dot(sc.astype(BF16), v, preferred_element_type=F32)
                     + jnp.dot((q * qdec_ref[h]).astype(BF16), state.astype(BF16),
                               preferred_element_type=F32))
                kd = (k * kdec_ref[h]).astype(BF16)
                rstate_ref[s, h] = (state * cdec_ref[h, 0:1, :]
                                    + lax.dot_general(kd, v, (((0,), (0,)), ((), ())),
                                                      preferred_element_type=F32))
                oc = o - jnp.mean(o, axis=-1, keepdims=True)
                on = oc * lax.rsqrt(jnp.mean(oc * oc, axis=-1, keepdims=True) + GN_EPS)
                ret = g * jax.nn.sigmoid(g) * (on * retw_ref[:, lo:lo + HEAD_DIM])
                mix_ref[r0:r0 + chunk, lo:lo + HEAD_DIM] = ret.astype(BF16)

        base = 4 * RET_WIDTH
        gate_b = proj[s * tm:(s + 1) * tm, base:base + CONV_WIDTH]
        gate_c = proj[s * tm:(s + 1) * tm, base + CONV_WIDTH:base + 2 * CONV_WIDTH]
        hc = proj[s * tm:(s + 1) * tm, base + 2 * CONV_WIDTH:base + 3 * CONV_WIDTH]
        u = gate_c * hc
        pad = F32_SUBLANES
        ubuf_ref[pad - (CONV_K - 1):pad, :] = cstate_ref[s]
        ubuf_ref[pad:pad + tm, :] = u
        z = (ubuf_ref[pad - 2:pad - 2 + tm, :] * convw_ref[0:1, :]
             + ubuf_ref[pad - 1:pad - 1 + tm, :] * convw_ref[1:2, :]
             + u * convw_ref[2:3, :])
        cstate_ref[s] = ubuf_ref[pad + tm - (CONV_K - 1):pad + tm, :]
        mix_ref[s * tm:(s + 1) * tm, RET_WIDTH:] = (gate_b * z).astype(BF16)

    return load_x() + jnp.dot(mix_ref[...], wout_ref[...], preferred_element_type=F32)


def _mlp_tile(load_x, ln2_ref, wup_ref, wdown_ref, lnf_ref, *, final_norm):
    h2 = _rms_norm(load_x(), ln2_ref[...]).astype(BF16)
    up = jnp.dot(h2, wup_ref[...], preferred_element_type=F32)
    act = jnp.square(jnp.maximum(up, 0.0)).astype(BF16)
    y = load_x() + jnp.dot(act, wdown_ref[...], preferred_element_type=F32)
    if final_norm:
        y = _rms_norm(y, lnf_ref[...])
    return y


def _prompt_layer_kernel(x_ref, ln1_ref, win_ref, cos_ref, sin_ref, dmat_ref, qdec_ref, kdec_ref,
                         cdec_ref, convw_ref, retw_ref, wout_ref, ln2_ref, wup_ref, wdown_ref, lnf_ref,
                         xo_ref, rnew_ref, cnew_ref, mix_ref, ubuf_ref, x1_ref,
                         *, n_tiles, tiles_per_seq, tm, chunk, final_norm):
    t = pl.program_id(0)

    @pl.when(jnp.logical_and(lax.rem(t, tiles_per_seq) == 0, t < n_tiles))
    def _():
        rnew_ref[...] = jnp.zeros(rnew_ref.shape, F32)
        cnew_ref[...] = jnp.zeros(cnew_ref.shape, F32)

    def mixer():
        return _mixer_tile(lambda: x_ref[0], ln1_ref, win_ref, cos_ref, sin_ref, dmat_ref, qdec_ref,
                           kdec_ref, cdec_ref, convw_ref, retw_ref, wout_ref, rnew_ref, cnew_ref,
                           mix_ref, ubuf_ref, nb=1, tm=tm, chunk=chunk)

    def mlp(slot):
        return _mlp_tile(lambda: x1_ref[slot], ln2_ref, wup_ref, wdown_ref, lnf_ref,
                         final_norm=final_norm)

    cur = lax.rem(t, 2)
    prev = 1 - cur

    @pl.when(t == 0)
    def _():
        x1_ref[cur] = mixer()

    @pl.when(jnp.logical_and(t > 0, t < n_tiles))
    def _():
        xo_ref[0] = mlp(prev)
        x1_ref[cur] = mixer()

    @pl.when(t == n_tiles)
    def _():
        xo_ref[0] = mlp(prev)


def _prompt_layer_call(x, layer, params, *, final_norm):
    b, l, _ = x.shape
    tm, chunk = PROMPT_TILE, PROMPT_CHUNK
    assert l % tm == 0 and tm % chunk == 0 and chunk % 16 == 0
    tiles_per_seq = l // tm
    n_tiles = b * tiles_per_seq
    cos2, sin2 = _rope_tables(0, l)
    dmat, qdec, kdec, cdec = _decay_tables(chunk)

    def tile_in(t):
        return jnp.minimum(t, n_tiles - 1)

    def tile_out(t):
        return jnp.maximum(t - 1, 0)

    const = lambda shape: pl.BlockSpec(shape, lambda t: (0,) * len(shape))
    per_layer = lambda shape: pl.BlockSpec((None,) + shape, lambda t: (layer,) + (0,) * len(shape))
    kernel = functools.partial(_prompt_layer_kernel, n_tiles=n_tiles, tiles_per_seq=tiles_per_seq,
                               tm=tm, chunk=chunk, final_norm=final_norm)
    return pl.pallas_call(
        kernel,
        grid=(n_tiles + 1,),
        in_specs=[
            pl.BlockSpec((1, tm, D_MODEL),
                         lambda t: (tile_in(t) // tiles_per_seq, tile_in(t) % tiles_per_seq, 0)),
            per_layer((1, D_MODEL)),
            per_layer((D_MODEL, IN_COLS)),
            pl.BlockSpec((tm, HEAD_DIM), lambda t: (tile_in(t) % tiles_per_seq, 0)),
            pl.BlockSpec((tm, HEAD_DIM), lambda t: (tile_in(t) % tiles_per_seq, 0)),
            const((N_HEADS, chunk, chunk)),
            const((N_HEADS, chunk, HEAD_DIM)),
            const((N_HEADS, chunk, HEAD_DIM)),
            const((N_HEADS, F32_SUBLANES, HEAD_DIM)),
            per_layer((CONV_K, CONV_WIDTH)),
            per_layer((1, RET_WIDTH)),
            per_layer((D_MODEL, D_MODEL)),
            per_layer((1, D_MODEL)),
            per_layer((D_MODEL, D_FF)),
            per_layer((D_FF, D_MODEL)),
            const((1, D_MODEL)),
        ],
        out_specs=[
            pl.BlockSpec((1, tm, D_MODEL),
                         lambda t: (tile_out(t) // tiles_per_seq, tile_out(t) % tiles_per_seq, 0)),
            pl.BlockSpec((1, N_HEADS, HEAD_DIM, HEAD_DIM),
                         lambda t: (tile_in(t) // tiles_per_seq, 0, 0, 0)),
            pl.BlockSpec((1, CONV_K - 1, CONV_WIDTH), lambda t: (tile_in(t) // tiles_per_seq, 0, 0)),
        ],
        out_shape=[
            jax.ShapeDtypeStruct((b, l, D_MODEL), F32),
            jax.ShapeDtypeStruct((b, N_HEADS, HEAD_DIM, HEAD_DIM), F32),
            jax.ShapeDtypeStruct((b, CONV_K - 1, CONV_WIDTH), F32),
        ],
        scratch_shapes=[
            pltpu.VMEM((tm, D_MODEL), BF16),
            pltpu.VMEM((F32_SUBLANES + tm, CONV_WIDTH), F32),
            pltpu.VMEM((2, tm, D_MODEL), F32),
        ],
        compiler_params=pltpu.CompilerParams(
            dimension_semantics=("arbitrary",),
            vmem_limit_bytes=V7X_VMEM_LIMIT_BYTES),
        name="prompt_layer",
    )(x, params["ln1"], params["w_in"], cos2, sin2, dmat, qdec, kdec, cdec, params["conv_w"],
      params["ret_w"], params["w_out"], params["ln2"], params["w_up"], params["w_down"], params["ln_f"])


def _sample_mixer_kernel(x_ref, ln1_ref, win_ref, cos_ref, sin_ref, dmat_ref, qdec_ref, kdec_ref,
                         cdec_ref, convw_ref, retw_ref, wout_ref, r0_ref, cb0_ref,
                         xo_ref, rnew_ref, cnew_ref, mix_ref, ubuf_ref, *, nb, tm):
    rnew_ref[...] = r0_ref[...]
    cnew_ref[...] = cb0_ref[...]
    y = _mixer_tile(lambda: x_ref[...].reshape(nb * tm, D_MODEL), ln1_ref, win_ref, cos_ref, sin_ref, dmat_ref,
                    qdec_ref, kdec_ref, cdec_ref, convw_ref, retw_ref, wout_ref, rnew_ref, cnew_ref,
                    mix_ref, ubuf_ref, nb=nb, tm=tm, chunk=tm)
    xo_ref[...] = y.reshape(nb, tm, D_MODEL)


def _sample_mixer_call(x, r_all, c_all, layer, params):
    nb, tm, _ = x.shape
    assert tm % 16 == 0
    cos2, sin2 = _rope_tables(PAST_LEN, tm)
    dmat, qdec, kdec, cdec = _decay_tables(tm)
    const = lambda shape: pl.BlockSpec(shape, lambda i: (0,) * len(shape))
    per_layer = lambda shape: pl.BlockSpec((None,) + shape, lambda i: (layer,) + (0,) * len(shape))
    return pl.pallas_call(
        functools.partial(_sample_mixer_kernel, nb=nb, tm=tm),
        grid=(1,),
        in_specs=[
            const((nb, tm, D_MODEL)),
            per_layer((1, D_MODEL)),
            per_layer((D_MODEL, IN_COLS)),
            const((tm, HEAD_DIM)),
            const((tm, HEAD_DIM)),
            const((N_HEADS, tm, tm)),
            const((N_HEADS, tm, HEAD_DIM)),
            const((N_HEADS, tm, HEAD_DIM)),
            const((N_HEADS, F32_SUBLANES, HEAD_DIM)),
            per_layer((CONV_K, CONV_WIDTH)),
            per_layer((1, RET_WIDTH)),
            per_layer((D_MODEL, D_MODEL)),
            per_layer((nb, N_HEADS, HEAD_DIM, HEAD_DIM)),
            per_layer((nb, CONV_K - 1, CONV_WIDTH)),
        ],
        out_specs=[
            const((nb, tm, D_MODEL)),
            const((nb, N_HEADS, HEAD_DIM, HEAD_DIM)),
            const((nb, CONV_K - 1, CONV_WIDTH)),
        ],
        out_shape=[
            jax.ShapeDtypeStruct((nb, tm, D_MODEL), F32),
            jax.ShapeDtypeStruct((nb, N_HEADS, HEAD_DIM, HEAD_DIM), F32),
            jax.ShapeDtypeStruct((nb, CONV_K - 1, CONV_WIDTH), F32),
        ],
        scratch_shapes=[
            pltpu.VMEM((nb * tm, D_MODEL), BF16),
            pltpu.VMEM((F32_SUBLANES + tm, CONV_WIDTH), F32),
        ],
        compiler_params=pltpu.CompilerParams(
            dimension_semantics=("arbitrary",),
            vmem_limit_bytes=V7X_VMEM_LIMIT_BYTES),
        name="sample_mixer",
    )(x, params["ln1"], params["w_in"], cos2, sin2, dmat, qdec, kdec, cdec, params["conv_w"],
      params["ret_w"], params["w_out"], r_all, c_all)


def _sample_mlp_kernel(x_ref, ln2_ref, wup_ref, wdown_ref, lnf_ref, o_ref, *, final_norm):
    o_ref[...] = _mlp_tile(lambda: x_ref[...], ln2_ref, wup_ref, wdown_ref, lnf_ref, final_norm=final_norm)


def _sample_mlp_call(x2d, layer, params, *, final_norm):
    t, _ = x2d.shape
    const = lambda shape: pl.BlockSpec(shape, lambda i: (0,) * len(shape))
    per_layer = lambda shape: pl.BlockSpec((None,) + shape, lambda i: (layer,) + (0,) * len(shape))
    return pl.pallas_call(
        functools.partial(_sample_mlp_kernel, final_norm=final_norm),
        grid=(1,),
        in_specs=[
            const((t, D_MODEL)),
            per_layer((1, D_MODEL)),
            per_layer((D_MODEL, D_FF)),
            per_layer((D_FF, D_MODEL)),
            const((1, D_MODEL)),
        ],
        out_specs=const((t, D_MODEL)),
        out_shape=jax.ShapeDtypeStruct((t, D_MODEL), F32),
        compiler_params=pltpu.CompilerParams(
            dimension_semantics=("arbitrary",),
            vmem_limit_bytes=V7X_VMEM_LIMIT_BYTES),
        name="sample_mlp",
    )(x2d, params["ln2"], params["w_up"], params["w_down"], params["ln_f"])


def kernel(x_prompt, x_sample, state_ret, state_conv, ln1_w, w_in, conv_w, ret_norm_w, w_out, ln2_w, w_mlp_up, w_mlp_down, ln_f_w):
    depth = w_in.shape[0]
    b_s, l_s, _ = x_sample.shape

    params = dict(
        ln1=ln1_w.reshape(depth, 1, D_MODEL),
        w_in=w_in.astype(BF16),
        conv_w=jnp.swapaxes(conv_w, 1, 2),
        ret_w=ret_norm_w.reshape(depth, 1, RET_WIDTH),
        w_out=w_out.astype(BF16),
        ln2=ln2_w.reshape(depth, 1, D_MODEL),
        w_up=w_mlp_up.astype(BF16),
        w_down=w_mlp_down.astype(BF16),
        ln_f=ln_f_w.reshape(1, D_MODEL),
    )

    xp, xs = x_prompt, x_sample
    ret_p, conv_p, ret_s, conv_s = [], [], [], []
    for i in range(depth):
        last = i == depth - 1
        xp, rp, cp = _prompt_layer_call(xp, i, params, final_norm=last)
        xs, rs, cs = _sample_mixer_call(xs, state_ret, state_conv, i, params)
        xs = _sample_mlp_call(xs.reshape(b_s * l_s, D_MODEL), i, params,
                              final_norm=last).reshape(b_s, l_s, D_MODEL)
        ret_p.append(rp)
        conv_p.append(cp)
        ret_s.append(rs)
        conv_s.append(cs)

    return (xp, xs, jnp.stack(ret_p), jnp.stack(conv_p), jnp.stack(ret_s), jnp.stack(conv_s))
```

```python
import functools
from typing import Any, NamedTuple

import numpy as np
import jax
import jax.numpy as jnp
from jax import lax
from jax.experimental import pallas as pl
from jax.experimental.pallas import tpu as pltpu

D_MODEL = 1024
N_HEADS = 4
HEAD_DIM = 128
RET_WIDTH = N_HEADS * HEAD_DIM
CONV_WIDTH = D_MODEL - RET_WIDTH
CONV_K = 3
D_FF = 4 * D_MODEL
IN_COLS = 4 * RET_WIDTH + 3 * CONV_WIDTH
PAST_LEN = 2048
ROPE_BASE = 10000.0
RMS_EPS = 1e-6
GN_EPS = 1e-5

V7X_VMEM_LIMIT_BYTES = 60 * 1024 * 1024
F32_SUBLANES = 8
PROMPT_TILE = 512
PROMPT_CHUNK = 256

BF16 = jnp.bfloat16
F32 = jnp.float32


def _rope_tables(pos0, length):
    half = HEAD_DIM // 2
    inv = ROPE_BASE ** (-np.arange(half, dtype=np.float64) / half)
    ang = (pos0 + np.arange(length, dtype=np.float64))[:, None] * inv[None, :]
    cos, sin = np.cos(ang), np.sin(ang)
    cos2 = np.concatenate([cos, cos], axis=-1)
    sin2 = np.concatenate([-sin, sin], axis=-1)
    return jnp.asarray(cos2, F32), jnp.asarray(sin2, F32)


def _decay_tables(chunk):
    lg = np.log(1.0 - np.exp2(-5.0 - np.arange(N_HEADS, dtype=np.float64)))
    idx = np.arange(chunk, dtype=np.float64)
    diff = idx[:, None] - idx[None, :]
    scale = HEAD_DIM ** -0.5
    dmat = np.where(diff >= 0, np.exp(np.maximum(diff, 0.0)[None] * lg[:, None, None]), 0.0) * scale
    qdec = np.exp((idx[None, :] + 1.0) * lg[:, None])
    kdec = np.exp((chunk - 1.0 - idx[None, :]) * lg[:, None]) * scale
    cdec = np.exp(chunk * lg)
    lanes = np.ones((1, 1, HEAD_DIM))
    return (jnp.asarray(dmat, F32),
            jnp.asarray(qdec[:, :, None] * lanes, F32),
            jnp.asarray(kdec[:, :, None] * lanes, F32),
            jnp.asarray(cdec[:, None, None] * np.ones((1, F32_SUBLANES, HEAD_DIM)), F32))


def _rms_norm(x, w):
    return x * lax.rsqrt(jnp.mean(x * x, axis=-1, keepdims=True) + RMS_EPS) * w


class MixerRefs(NamedTuple):
    ln1: Any
    w_in: Any
    cos: Any
    sin: Any
    dmat: Any
    qdec: Any
    kdec: Any
    cdec: Any
    conv_w: Any
    ret_w: Any
    w_out: Any


class MlpRefs(NamedTuple):
    ln2: Any
    w_up: Any
    w_down: Any
    ln_f: Any


def _rope(t, cos, sin):
    return t * cos + pltpu.roll(t, HEAD_DIM // 2, axis=1) * sin


def _retention_block(proj, lo_row, out_row, s, c, p, rstate_ref, mix_ref, *, chunk):
    cos = p.cos[c * chunk:(c + 1) * chunk, :]
    sin = p.sin[c * chunk:(c + 1) * chunk, :]
    heads = range(N_HEADS)
    col = lambda part, h: proj[lo_row:lo_row + chunk, part * RET_WIDTH + h * HEAD_DIM:
                               part * RET_WIDTH + (h + 1) * HEAD_DIM]
    qs = [_rope(col(0, h), cos, sin) for h in heads]
    ks = [_rope(col(1, h), cos, sin) for h in heads]
    vs = [col(2, h).astype(BF16) for h in heads]
    scs = [lax.dot_general(qs[h].astype(BF16), ks[h].astype(BF16), (((1,), (1,)), ((), ())),
                           preferred_element_type=F32) for h in heads]
    states = [rstate_ref[s, h] for h in heads]
    os = [jnp.dot((scs[h] * p.dmat[h]).astype(BF16), vs[h], preferred_element_type=F32)
          + jnp.dot((qs[h] * p.qdec[h]).astype(BF16), states[h].astype(BF16),
                    preferred_element_type=F32) for h in heads]
    for h in heads:
        kd = (ks[h] * p.kdec[h]).astype(BF16)
        rstate_ref[s, h] = (states[h] * p.cdec[h, 0:1, :]
                            + lax.dot_general(kd, vs[h], (((0,), (0,)), ((), ())),
                                              preferred_element_type=F32))
    for h in heads:
        lo = h * HEAD_DIM
        o, g = os[h], col(3, h)
        oc = o - jnp.mean(o, axis=-1, keepdims=True)
        on = oc * lax.rsqrt(jnp.mean(oc * oc, axis=-1, keepdims=True) + GN_EPS)
        ret = g * jax.nn.sigmoid(g) * (on * p.ret_w[:, lo:lo + HEAD_DIM])
        mix_ref[out_row:out_row + chunk, lo:lo + HEAD_DIM] = ret.astype(BF16)


def _conv_block(proj, lo_row, out_row, s, p, cstate_ref, mix_ref, ubuf_ref, *, chunk):
    base = 4 * RET_WIDTH
    gate_b = proj[lo_row:lo_row + chunk, base:base + CONV_WIDTH]
    gate_c = proj[lo_row:lo_row + chunk, base + CONV_WIDTH:base + 2 * CONV_WIDTH]
    hc = proj[lo_row:lo_row + chunk, base + 2 * CONV_WIDTH:base + 3 * CONV_WIDTH]
    u = gate_c * hc
    pad = F32_SUBLANES
    ubuf_ref[pad - (CONV_K - 1):pad, :] = cstate_ref[s]
    ubuf_ref[pad:pad + chunk, :] = u
    z = (ubuf_ref[pad - 2:pad - 2 + chunk, :] * p.conv_w[0:1, :]
         + ubuf_ref[pad - 1:pad - 1 + chunk, :] * p.conv_w[1:2, :]
         + u * p.conv_w[2:3, :])
    cstate_ref[s] = ubuf_ref[pad + chunk - (CONV_K - 1):pad + chunk, :]
    mix_ref[out_row:out_row + chunk, RET_WIDTH:] = (gate_b * z).astype(BF16)


def _mixer_stages(load_x, store_y, p, rstate_ref, cstate_ref, mix_ref, ubuf_ref,
                  *, nb, tm, chunk, n_groups):
    rows = nb * tm
    g_rows = rows // n_groups
    assert rows % n_groups == 0 and g_rows % chunk == 0 and tm % chunk == 0
    projs = []
    for g in range(n_groups):
        n = _rms_norm(load_x(g * g_rows, (g + 1) * g_rows), p.ln1[...]).astype(BF16)
        projs.append(jnp.dot(n, p.w_in[...], preferred_element_type=F32))
        yield
    for g in range(n_groups):
        r0 = g * g_rows
        for lo_row in range(0, g_rows, chunk):
            s, c = divmod(r0 + lo_row, tm)
            c //= chunk
            _retention_block(projs[g], lo_row, r0 + lo_row, s, c, p, rstate_ref, mix_ref, chunk=chunk)
            _conv_block(projs[g], lo_row, r0 + lo_row, s, p, cstate_ref, mix_ref, ubuf_ref, chunk=chunk)
        yield
        store_y(r0, r0 + g_rows, load_x(r0, r0 + g_rows)
                + jnp.dot(mix_ref[r0:r0 + g_rows, :], p.w_out[...], preferred_element_type=F32))
        if g + 1 < n_groups:
            yield


def _mlp_stages(load_x, store_y, p, *, rows, n_groups, final_norm):
    g_rows = rows // n_groups
    assert rows % n_groups == 0
    ups = []
    for g in range(n_groups):
        h2 = _rms_norm(load_x(g * g_rows, (g + 1) * g_rows), p.ln2[...]).astype(BF16)
        ups.append(jnp.dot(h2, p.w_up[...], preferred_element_type=F32))
        yield
    for g in range(n_groups):
        r0 = g * g_rows
        act = jnp.square(jnp.maximum(ups[g], 0.0)).astype(BF16)
        y = load_x(r0, r0 + g_rows) + jnp.dot(act, p.w_down[...], preferred_element_type=F32)
        if final_norm:
            y = _rms_norm(y, p.ln_f[...])
        store_y(r0, r0 + g_rows, y)
        if g + 1 < n_groups:
            yield


def _run_stages(order, **stages):
    for name in order:
        next(stages[name], None)
    for gen in stages.values():
        assert next(gen, "done") == "done"


FUSED_STAGE_ORDER = "MLMLMLMMLM"


def _prompt_layer_kernel(x_ref, ln1_ref, win_ref, cos_ref, sin_ref, dmat_ref, qdec_ref, kdec_ref,
                         cdec_ref, convw_ref, retw_ref, wout_ref, ln2_ref, wup_ref, wdown_ref, lnf_ref,
                         *rest, n_tiles, tiles_per_seq, tm, chunk, final_norm, stage_order):
    xo_ref, rnew_ref, cnew_ref, mix_ref, ubuf_ref, x1_ref = rest[-6:]
    t = pl.program_id(0)
    mixer_refs = MixerRefs(ln1_ref, win_ref, cos_ref, sin_ref, dmat_ref, qdec_ref, kdec_ref, cdec_ref,
                           convw_ref, retw_ref, wout_ref)
    mlp_refs = MlpRefs(ln2_ref, wup_ref, wdown_ref, lnf_ref)

    @pl.when(jnp.logical_and(lax.rem(t, tiles_per_seq) == 0, t < n_tiles))
    def _():
        rnew_ref[...] = jnp.zeros(rnew_ref.shape, F32)
        cnew_ref[...] = jnp.zeros(cnew_ref.shape, F32)

    cur = lax.rem(t, 2)
    prev = 1 - cur

    def store_x1(r0, r1, y):
        x1_ref[cur, r0:r1, :] = y

    def store_out(r0, r1, y):
        xo_ref[r0:r1, :] = y

    def mixer():
        return _mixer_stages(lambda r0, r1: x_ref[r0:r1, :], store_x1, mixer_refs, rnew_ref, cnew_ref,
                             mix_ref, ubuf_ref, nb=1, tm=tm, chunk=chunk, n_groups=tm // chunk)

    def mlp():
        return _mlp_stages(lambda r0, r1: x1_ref[prev, r0:r1, :], store_out, mlp_refs,
                           rows=tm, n_groups=tm // chunk, final_norm=final_norm)

    @pl.when(t == 0)
    def _():
        _run_stages("M" * 6, M=mixer())

    @pl.when(jnp.logical_and(t > 0, t < n_tiles))
    def _():
        _run_stages(stage_order, M=mixer(), L=mlp())

    @pl.when(t == n_tiles)
    def _():
        _run_stages("L" * 4, L=mlp())


def _stacked_state_plumbing(prev_states, n_fixed_inputs):
    extra_specs = [pl.BlockSpec(memory_space=pl.ANY) for _ in prev_states]
    aliases = {n_fixed_inputs + j: 1 + j for j in range(len(prev_states))}
    return extra_specs, aliases


def _prompt_layer_call(x2d, seq_len, layer, depth, params, prev_states, *, final_norm,
                       stage_order=FUSED_STAGE_ORDER):
    rows, _ = x2d.shape
    tm, chunk = PROMPT_TILE, PROMPT_CHUNK
    assert tm == 2 * chunk, "FUSED_STAGE_ORDER is written for two row groups per tile"
    assert seq_len % tm == 0 and rows % seq_len == 0 and chunk % 16 == 0
    tiles_per_seq = seq_len // tm
    n_tiles = rows // tm
    n_seq = rows // seq_len
    cos2, sin2 = _rope_tables(0, seq_len)
    dmat, qdec, kdec, cdec = _decay_tables(chunk)

    def tile_in(t):
        return jnp.minimum(t, n_tiles - 1)

    def tile_out(t):
        return jnp.maximum(t - 1, 0)

    const = lambda shape: pl.BlockSpec(shape, lambda t: (0,) * len(shape))
    per_layer = lambda shape: pl.BlockSpec((None,) + shape, lambda t: (layer,) + (0,) * len(shape))
    kernel = functools.partial(_prompt_layer_kernel, n_tiles=n_tiles, tiles_per_seq=tiles_per_seq,
                               tm=tm, chunk=chunk, final_norm=final_norm, stage_order=stage_order)
    extra_specs, aliases = _stacked_state_plumbing(prev_states, n_fixed_inputs=16)
    return pl.pallas_call(
        kernel,
        grid=(n_tiles + 1,),
        in_specs=[
            pl.BlockSpec((tm, D_MODEL), lambda t: (tile_in(t), 0)),
            per_layer((1, D_MODEL)),
            per_layer((D_MODEL, IN_COLS)),
            pl.BlockSpec((tm, HEAD_DIM), lambda t: (tile_in(t) % tiles_per_seq, 0)),
            pl.BlockSpec((tm, HEAD_DIM), lambda t: (tile_in(t) % tiles_per_seq, 0)),
            const((N_HEADS, chunk, chunk)),
            const((N_HEADS, chunk, HEAD_DIM)),
            const((N_HEADS, chunk, HEAD_DIM)),
            const((N_HEADS, F32_SUBLANES, HEAD_DIM)),
            per_layer((CONV_K, CONV_WIDTH)),
            per_layer((1, RET_WIDTH)),
            per_layer((D_MODEL, D_MODEL)),
            per_layer((1, D_MODEL)),
            per_layer((D_MODEL, D_FF)),
            per_layer((D_FF, D_MODEL)),
            const((1, D_MODEL)),
        ] + extra_specs,
        out_specs=[
            pl.BlockSpec((tm, D_MODEL), lambda t: (tile_out(t), 0)),
            pl.BlockSpec((None, 1, N_HEADS, HEAD_DIM, HEAD_DIM),
                         lambda t: (layer, tile_in(t) // tiles_per_seq, 0, 0, 0)),
            pl.BlockSpec((None, 1, CONV_K - 1, CONV_WIDTH),
                         lambda t: (layer, tile_in(t) // tiles_per_seq, 0, 0)),
        ],
        out_shape=[
            jax.ShapeDtypeStruct((rows, D_MODEL), F32),
            jax.ShapeDtypeStruct((depth, n_seq, N_HEADS, HEAD_DIM, HEAD_DIM), F32),
            jax.ShapeDtypeStruct((depth, n_seq, CONV_K - 1, CONV_WIDTH), F32),
        ],
        input_output_aliases=aliases,
        scratch_shapes=[
            pltpu.VMEM((tm, D_MODEL), BF16),
            pltpu.VMEM((F32_SUBLANES + chunk, CONV_WIDTH), F32),
            pltpu.VMEM((2, tm, D_MODEL), F32),
        ],
        compiler_params=pltpu.CompilerParams(
            dimension_semantics=("arbitrary",),
            vmem_limit_bytes=V7X_VMEM_LIMIT_BYTES),
        name="prompt_layer",
    )(x2d, params["ln1"], params["w_in"], cos2, sin2, dmat, qdec, kdec, cdec, params["conv_w"],
      params["ret_w"], params["w_out"], params["ln2"], params["w_up"], params["w_down"], params["ln_f"],
      *prev_states)


def _sample_mixer_kernel(x_ref, ln1_ref, win_ref, cos_ref, sin_ref, dmat_ref, qdec_ref, kdec_ref,
                         cdec_ref, convw_ref, retw_ref, wout_ref, r0_ref, cb0_ref, *rest, nb, tm):
    xo_ref, rnew_ref, cnew_ref, mix_ref, ubuf_ref = rest[-5:]
    mixer_refs = MixerRefs(ln1_ref, win_ref, cos_ref, sin_ref, dmat_ref, qdec_ref, kdec_ref, cdec_ref,
                           convw_ref, retw_ref, wout_ref)
    rnew_ref[...] = r0_ref[...]
    cnew_ref[...] = cb0_ref[...]

    def store_out(r0, r1, y):
        xo_ref[r0:r1, :] = y

    _run_stages("M" * 3, M=_mixer_stages(lambda r0, r1: x_ref[r0:r1, :], store_out, mixer_refs, rnew_ref,
                                         cnew_ref, mix_ref, ubuf_ref, nb=nb, tm=tm, chunk=tm, n_groups=1))


def _sample_mixer_call(x2d, r_all, c_all, layer, depth, params, prev_states, *, nb, tm):
    assert tm % 16 == 0 and x2d.shape[0] == nb * tm
    extra_specs, aliases = _stacked_state_plumbing(prev_states, n_fixed_inputs=14)
    cos2, sin2 = _rope_tables(PAST_LEN, tm)
    dmat, qdec, kdec, cdec = _decay_tables(tm)
    const = lambda shape: pl.BlockSpec(shape, lambda i: (0,) * len(shape))
    per_layer = lambda shape: pl.BlockSpec((None,) + shape, lambda i: (layer,) + (0,) * len(shape))
    return pl.pallas_call(
        functools.partial(_sample_mixer_kernel, nb=nb, tm=tm),
        grid=(1,),
        in_specs=[
            const((nb * tm, D_MODEL)),
            per_layer((1, D_MODEL)),
            per_layer((D_MODEL, IN_COLS)),
            const((tm, HEAD_DIM)),
            const((tm, HEAD_DIM)),
            const((N_HEADS, tm, tm)),
            const((N_HEADS, tm, HEAD_DIM)),
            const((N_HEADS, tm, HEAD_DIM)),
            const((N_HEADS, F32_SUBLANES, HEAD_DIM)),
            per_layer((CONV_K, CONV_WIDTH)),
            per_layer((1, RET_WIDTH)),
            per_layer((D_MODEL, D_MODEL)),
            per_layer((nb, N_HEADS, HEAD_DIM, HEAD_DIM)),
            per_layer((nb, CONV_K - 1, CONV_WIDTH)),
        ] + extra_specs,
        out_specs=[
            const((nb * tm, D_MODEL)),
            per_layer((nb, N_HEADS, HEAD_DIM, HEAD_DIM)),
            per_layer((nb, CONV_K - 1, CONV_WIDTH)),
        ],
        out_shape=[
            jax.ShapeDtypeStruct((nb * tm, D_MODEL), F32),
            jax.ShapeDtypeStruct((depth, nb, N_HEADS, HEAD_DIM, HEAD_DIM), F32),
            jax.ShapeDtypeStruct((depth, nb, CONV_K - 1, CONV_WIDTH), F32),
        ],
        input_output_aliases=aliases,
        scratch_shapes=[
            pltpu.VMEM((nb * tm, D_MODEL), BF16),
            pltpu.VMEM((F32_SUBLANES + tm, CONV_WIDTH), F32),
        ],
        compiler_params=pltpu.CompilerParams(
            dimension_semantics=("arbitrary",),
            vmem_limit_bytes=V7X_VMEM_LIMIT_BYTES),
        name="sample_mixer",
    )(x2d, params["ln1"], params["w_in"], cos2, sin2, dmat, qdec, kdec, cdec, params["conv_w"],
      params["ret_w"], params["w_out"], r_all, c_all, *prev_states)


def _sample_mlp_kernel(x_ref, ln2_ref, wup_ref, wdown_ref, lnf_ref, o_ref, *, final_norm):
    def store_out(r0, r1, y):
        o_ref[r0:r1, :] = y

    _run_stages("L" * 2, L=_mlp_stages(lambda r0, r1: x_ref[r0:r1, :], store_out,
                                       MlpRefs(ln2_ref, wup_ref, wdown_ref, lnf_ref),
                                       rows=x_ref.shape[0], n_groups=1, final_norm=final_norm))


def _sample_mlp_call(x2d, layer, params, *, final_norm):
    t, _ = x2d.shape
    const = lambda shape: pl.BlockSpec(shape, lambda i: (0,) * len(shape))
    per_layer = lambda shape: pl.BlockSpec((None,) + shape, lambda i: (layer,) + (0,) * len(shape))
    return pl.pallas_call(
        functools.partial(_sample_mlp_kernel, final_norm=final_norm),
        grid=(1,),
        in_specs=[
            const((t, D_MODEL)),
            per_layer((1, D_MODEL)),
            per_layer((D_MODEL, D_FF)),
            per_layer((D_FF, D_MODEL)),
            const((1, D_MODEL)),
        ],
        out_specs=const((t, D_MODEL)),
        out_shape=jax.ShapeDtypeStruct((t, D_MODEL), F32),
        compiler_params=pltpu.CompilerParams(
            dimension_semantics=("arbitrary",),
            vmem_limit_bytes=V7X_VMEM_LIMIT_BYTES),
        name="sample_mlp",
    )(x2d, params["ln2"], params["w_up"], params["w_down"], params["ln_f"])


def kernel(x_prompt, x_sample, state_ret, state_conv, ln1_w, w_in, conv_w, ret_norm_w, w_out, ln2_w, w_mlp_up, w_mlp_down, ln_f_w):
    depth = w_in.shape[0]
    b_p, l_p, _ = x_prompt.shape
    b_s, l_s, _ = x_sample.shape

    params = dict(
        ln1=ln1_w.reshape(depth, 1, D_MODEL),
        w_in=w_in.astype(BF16),
        conv_w=jnp.swapaxes(conv_w, 1, 2),
        ret_w=ret_norm_w.reshape(depth, 1, RET_WIDTH),
        w_out=w_out.astype(BF16),
        ln2=ln2_w.reshape(depth, 1, D_MODEL),
        w_up=w_mlp_up.astype(BF16),
        w_down=w_mlp_down.astype(BF16),
        ln_f=ln_f_w.reshape(1, D_MODEL),
    )

    xp = x_prompt.reshape(b_p * l_p, D_MODEL)
    xs = x_sample.reshape(b_s * l_s, D_MODEL)
    states_p, states_s = (), ()
    for i in range(depth):
        last = i == depth - 1
        xp, *states_p = _prompt_layer_call(xp, l_p, i, depth, params, states_p, final_norm=last)
        xs, *states_s = _sample_mixer_call(xs, state_ret, state_conv, i, depth, params, states_s,
                                           nb=b_s, tm=l_s)
        xs = _sample_mlp_call(xs, i, params, final_norm=last)

    return (xp.reshape(b_p, l_p, D_MODEL), xs.reshape(b_s, l_s, D_MODEL),
            states_p[0], states_p[1], states_s[0], states_s[1])
```

```python
import functools
from typing import Any, NamedTuple

import numpy as np
import jax
import jax.numpy as jnp
from jax import lax
from jax.experimental import pallas as pl
from jax.experimental.pallas import tpu as pltpu

D_MODEL = 1024
N_HEADS = 4
HEAD_DIM = 128
RET_WIDTH = N_HEADS * HEAD_DIM
CONV_WIDTH = D_MODEL - RET_WIDTH
CONV_K = 3
D_FF = 4 * D_MODEL
IN_COLS = 4 * RET_WIDTH + 3 * CONV_WIDTH
PAST_LEN = 2048
ROPE_BASE = 10000.0
RMS_EPS = 1e-6
GN_EPS = 1e-5

V7X_VMEM_LIMIT_BYTES = 60 * 1024 * 1024
F32_SUBLANES = 8
PROMPT_TILE = 512
PROMPT_CHUNK = 256

BF16 = jnp.bfloat16
F32 = jnp.float32


def _rope_tables(pos0, length):
    half = HEAD_DIM // 2
    inv = ROPE_BASE ** (-np.arange(half, dtype=np.float64) / half)
    ang = (pos0 + np.arange(length, dtype=np.float64))[:, None] * inv[None, :]
    cos, sin = np.cos(ang), np.sin(ang)
    cos2 = np.concatenate([cos, cos], axis=-1)
    sin2 = np.concatenate([-sin, sin], axis=-1)
    return jnp.asarray(cos2, F32), jnp.asarray(sin2, F32)


def _decay_tables(chunk):
    lg = np.log(1.0 - np.exp2(-5.0 - np.arange(N_HEADS, dtype=np.float64)))
    idx = np.arange(chunk, dtype=np.float64)
    diff = idx[:, None] - idx[None, :]
    scale = HEAD_DIM ** -0.5
    dmat = np.where(diff >= 0, np.exp(np.maximum(diff, 0.0)[None] * lg[:, None, None]), 0.0) * scale
    qdec = np.exp((idx[None, :] + 1.0) * lg[:, None])
    kdec = np.exp((chunk - 1.0 - idx[None, :]) * lg[:, None]) * scale
    cdec = np.exp(chunk * lg)
    lanes = np.ones((1, 1, HEAD_DIM))
    return (jnp.asarray(dmat, F32),
            jnp.asarray(qdec[:, :, None] * lanes, F32),
            jnp.asarray(kdec[:, :, None] * lanes, F32),
            jnp.asarray(cdec[:, None, None] * np.ones((1, F32_SUBLANES, HEAD_DIM)), F32))


def _rms_norm(x, w):
    return x * lax.rsqrt(jnp.mean(x * x, axis=-1, keepdims=True) + RMS_EPS) * w


class MixerRefs(NamedTuple):
    ln1: Any
    w_in: Any
    cos: Any
    sin: Any
    dmat: Any
    qdec: Any
    kdec: Any
    cdec: Any
    conv_w: Any
    ret_w: Any
    w_out: Any


class MlpRefs(NamedTuple):
    ln2: Any
    w_up: Any
    w_down: Any
    ln_f: Any


def _rope(t, cos, sin):
    return t * cos + pltpu.roll(t, HEAD_DIM // 2, axis=1) * sin


def _retention_block(proj, lo_row, out_row, s, c, p, rstate_ref, mix_ref, *, chunk):
    cos = p.cos[c * chunk:(c + 1) * chunk, :]
    sin = p.sin[c * chunk:(c + 1) * chunk, :]
    heads = range(N_HEADS)
    col = lambda part, h: proj[lo_row:lo_row + chunk, part * RET_WIDTH + h * HEAD_DIM:
                               part * RET_WIDTH + (h + 1) * HEAD_DIM]
    qs = [_rope(col(0, h), cos, sin) for h in heads]
    ks = [_rope(col(1, h), cos, sin) for h in heads]
    vs = [col(2, h).astype(BF16) for h in heads]
    scs = [lax.dot_general(qs[h].astype(BF16), ks[h].astype(BF16), (((1,), (1,)), ((), ())),
                           preferred_element_type=F32) for h in heads]
    states = [rstate_ref[s, h] for h in heads]
    os = [jnp.dot((scs[h] * p.dmat[h]).astype(BF16), vs[h], preferred_element_type=F32)
          + jnp.dot((qs[h] * p.qdec[h]).astype(BF16), states[h].astype(BF16),
                    preferred_element_type=F32) for h in heads]
    for h in heads:
        kd = (ks[h] * p.kdec[h]).astype(BF16)
        rstate_ref[s, h] = (states[h] * p.cdec[h, 0:1, :]
                            + lax.dot_general(kd, vs[h], (((0,), (0,)), ((), ())),
                                              preferred_element_type=F32))
    for h in heads:
        lo = h * HEAD_DIM
        o, g = os[h], col(3, h)
        oc = o - jnp.mean(o, axis=-1, keepdims=True)
        on = oc * lax.rsqrt(jnp.mean(oc * oc, axis=-1, keepdims=True) + GN_EPS)
        ret = g * jax.nn.sigmoid(g) * (on * p.ret_w[:, lo:lo + HEAD_DIM])
        mix_ref[out_row:out_row + chunk, lo:lo + HEAD_DIM] = ret.astype(BF16)


def _conv_block(proj, lo_row, out_row, s, p, cstate_ref, mix_ref, ubuf_ref, *, chunk):
    base = 4 * RET_WIDTH
    gate_b = proj[lo_row:lo_row + chunk, base:base + CONV_WIDTH]
    gate_c = proj[lo_row:lo_row + chunk, base + CONV_WIDTH:base + 2 * CONV_WIDTH]
    hc = proj[lo_row:lo_row + chunk, base + 2 * CONV_WIDTH:base + 3 * CONV_WIDTH]
    u = gate_c * hc
    pad = F32_SUBLANES
    ubuf_ref[pad - (CONV_K - 1):pad, :] = cstate_ref[s]
    ubuf_ref[pad:pad + chunk, :] = u
    z = (ubuf_ref[pad - 2:pad - 2 + chunk, :] * p.conv_w[0:1, :]
         + ubuf_ref[pad - 1:pad - 1 + chunk, :] * p.conv_w[1:2, :]
         + u * p.conv_w[2:3, :])
    cstate_ref[s] = ubuf_ref[pad + chunk - (CONV_K - 1):pad + chunk, :]
    mix_ref[out_row:out_row + chunk, RET_WIDTH:] = (gate_b * z).astype(BF16)


def _mixer_stages(load_x, store_y, p, rstate_ref, cstate_ref, mix_ref, ubuf_ref,
                  *, nb, tm, chunk, n_groups):
    rows = nb * tm
    g_rows = rows // n_groups
    assert rows % n_groups == 0 and g_rows % chunk == 0 and tm % chunk == 0
    projs = []
    for g in range(n_groups):
        n = _rms_norm(load_x(g * g_rows, (g + 1) * g_rows), p.ln1[...]).astype(BF16)
        projs.append(jnp.dot(n, p.w_in[...], preferred_element_type=F32))
        yield
    for g in range(n_groups):
        r0 = g * g_rows
        for lo_row in range(0, g_rows, chunk):
            s, c = divmod(r0 + lo_row, tm)
            c //= chunk
            _retention_block(projs[g], lo_row, r0 + lo_row, s, c, p, rstate_ref, mix_ref, chunk=chunk)
            _conv_block(projs[g], lo_row, r0 + lo_row, s, p, cstate_ref, mix_ref, ubuf_ref, chunk=chunk)
        yield
        store_y(r0, r0 + g_rows, load_x(r0, r0 + g_rows)
                + jnp.dot(mix_ref[r0:r0 + g_rows, :], p.w_out[...], preferred_element_type=F32))
        if g + 1 < n_groups:
            yield


def _mlp_stages(load_x, store_y, p, *, rows, n_groups):
    g_rows = rows // n_groups
    assert rows % n_groups == 0
    ups = []
    for g in range(n_groups):
        h2 = _rms_norm(load_x(g * g_rows, (g + 1) * g_rows), p.ln2[...]).astype(BF16)
        ups.append(jnp.dot(h2, p.w_up[...], preferred_element_type=F32))
        yield
    for g in range(n_groups):
        r0 = g * g_rows
        act = jnp.square(jnp.maximum(ups[g], 0.0)).astype(BF16)
        store_y(r0, r0 + g_rows, load_x(r0, r0 + g_rows)
                + jnp.dot(act, p.w_down[...], preferred_element_type=F32))
        if g + 1 < n_groups:
            yield


def _final_norm_stages(load_y, store_y, lnf_ref, *, rows, n_groups):
    g_rows = rows // n_groups
    for g in range(n_groups):
        r0 = g * g_rows
        store_y(r0, r0 + g_rows, _rms_norm(load_y(r0, r0 + g_rows), lnf_ref[...]))
        if g + 1 < n_groups:
            yield


def _run_stages(order, **stages):
    for name in order:
        next(stages[name], None)
    for gen in stages.values():
        assert next(gen, "done") == "done"


def _run_all(stages):
    for _ in stages:
        pass


FUSED_STAGE_ORDER = "MLMLMLMMLM"
FUSED_STAGE_ORDER_FINAL_NORM = "MLMLNMLMNMLM"
MLP_NORM_STAGE_ORDER = "LLNLNL"


def _prompt_layer_kernel(x_ref, ln1_ref, win_ref, cos_ref, sin_ref, dmat_ref, qdec_ref, kdec_ref,
                         cdec_ref, convw_ref, retw_ref, wout_ref, ln2_ref, wup_ref, wdown_ref, lnf_ref,
                         *rest, n_tiles, tiles_per_seq, tm, chunk, final_norm, stage_order, state_layer):
    if final_norm:
        xo_ref, rout_ref, cout_ref, mix_ref, ubuf_ref, x1_ref, yraw_ref = rest[-7:]
    else:
        xo_ref, rout_ref, cout_ref, mix_ref, ubuf_ref, x1_ref = rest[-6:]
    t = pl.program_id(0)
    mixer_refs = MixerRefs(ln1_ref, win_ref, cos_ref, sin_ref, dmat_ref, qdec_ref, kdec_ref, cdec_ref,
                           convw_ref, retw_ref, wout_ref)
    mlp_refs = MlpRefs(ln2_ref, wup_ref, wdown_ref, lnf_ref)

    @pl.when(jnp.logical_and(lax.rem(t, tiles_per_seq) == 0, t < n_tiles))
    def _():
        rout_ref[...] = jnp.zeros(rout_ref.shape, F32)
        cout_ref[...] = jnp.zeros(cout_ref.shape, F32)

    rnew_ref, cnew_ref = _own_layer_state(rout_ref, cout_ref, state_layer)

    cur = lax.rem(t, 2)
    prev = 1 - cur

    def store_x1(r0, r1, y):
        x1_ref[cur, r0:r1, :] = y

    def store_out(r0, r1, y):
        xo_ref[r0:r1, :] = y

    def store_raw(r0, r1, y):
        yraw_ref[r0:r1, :] = y

    n_groups = tm // chunk

    def mixer():
        return _mixer_stages(lambda r0, r1: x_ref[r0:r1, :], store_x1, mixer_refs, rnew_ref, cnew_ref,
                             mix_ref, ubuf_ref, nb=1, tm=tm, chunk=chunk, n_groups=n_groups)

    def mlp():
        return _mlp_stages(lambda r0, r1: x1_ref[prev, r0:r1, :], store_raw if final_norm else store_out,
                           mlp_refs, rows=tm, n_groups=n_groups)

    def norm():
        return _final_norm_stages(lambda r0, r1: yraw_ref[r0:r1, :], store_out, lnf_ref,
                                  rows=tm, n_groups=n_groups)

    if not final_norm:
        @pl.when(t == 0)
        def _():
            _run_all(mixer())

        @pl.when(jnp.logical_and(t > 0, t < n_tiles))
        def _():
            _run_stages(stage_order, M=mixer(), L=mlp())

        @pl.when(t == n_tiles)
        def _():
            _run_all(mlp())
    else:
        @pl.when(t == 0)
        def _():
            yraw_ref[...] = jnp.zeros(yraw_ref.shape, F32)
            _run_all(mixer())

        @pl.when(jnp.logical_and(t > 0, t < n_tiles))
        def _():
            _run_stages(stage_order, M=mixer(), L=mlp(), N=norm())

        @pl.when(t == n_tiles)
        def _():
            _run_stages(MLP_NORM_STAGE_ORDER, L=mlp(), N=norm())

        @pl.when(t == n_tiles + 1)
        def _():
            _run_all(norm())


def _stacked_state_plumbing(prev_states, n_fixed_inputs, layer, depth, state_block, state_index):
    if not prev_states:
        assert layer == 0
        out_specs = [pl.BlockSpec((depth,) + state_block(k), lambda i, k=k: (0,) + state_index(k, i))
                     for k in ("ret", "conv")]
        return [], out_specs, {}, layer
    out_specs = [pl.BlockSpec((None,) + state_block(k), lambda i, k=k: (layer,) + state_index(k, i))
                 for k in ("ret", "conv")]
    extra_specs = [pl.BlockSpec(memory_space=pl.ANY) for _ in prev_states]
    aliases = {n_fixed_inputs + j: 1 + j for j in range(len(prev_states))}
    return extra_specs, out_specs, aliases, None


def _own_layer_state(rout_ref, cout_ref, state_layer):
    if state_layer is None:
        return rout_ref, cout_ref
    return rout_ref.at[state_layer], cout_ref.at[state_layer]


def _prompt_layer_call(x2d, seq_len, layer, depth, params, prev_states, *, final_norm, stage_order=None):
    if stage_order is None:
        stage_order = FUSED_STAGE_ORDER_FINAL_NORM if final_norm else FUSED_STAGE_ORDER
    rows, _ = x2d.shape
    tm, chunk = PROMPT_TILE, PROMPT_CHUNK
    assert tm == 2 * chunk, "FUSED_STAGE_ORDER is written for two row groups per tile"
    assert seq_len % tm == 0 and rows % seq_len == 0 and chunk % 16 == 0
    tiles_per_seq = seq_len // tm
    n_tiles = rows // tm
    n_seq = rows // seq_len
    cos2, sin2 = _rope_tables(0, seq_len)
    dmat, qdec, kdec, cdec = _decay_tables(chunk)

    def tile_in(t):
        return jnp.minimum(t, n_tiles - 1)

    out_lag = 2 if final_norm else 1

    def tile_out(t):
        return jnp.clip(t - out_lag, 0, n_tiles - 1)

    const = lambda shape: pl.BlockSpec(shape, lambda t: (0,) * len(shape))
    per_layer = lambda shape: pl.BlockSpec((None,) + shape, lambda t: (layer,) + (0,) * len(shape))
    state_blocks = {"ret": (1, N_HEADS, HEAD_DIM, HEAD_DIM), "conv": (1, CONV_K - 1, CONV_WIDTH)}
    extra_specs, state_specs, aliases, state_layer = _stacked_state_plumbing(
        prev_states, 16, layer, depth, state_blocks.get,
        lambda kind, t: (tile_in(t) // tiles_per_seq,) + (0,) * (len(state_blocks[kind]) - 1))
    kernel = functools.partial(_prompt_layer_kernel, n_tiles=n_tiles, tiles_per_seq=tiles_per_seq,
                               tm=tm, chunk=chunk, final_norm=final_norm, stage_order=stage_order,
                               state_layer=state_layer)
    return pl.pallas_call(
        kernel,
        grid=(n_tiles + out_lag,),
        in_specs=[
            pl.BlockSpec((tm, D_MODEL), lambda t: (tile_in(t), 0)),
            per_layer((1, D_MODEL)),
            per_layer((D_MODEL, IN_COLS)),
            pl.BlockSpec((tm, HEAD_DIM), lambda t: (tile_in(t) % tiles_per_seq, 0)),
            pl.BlockSpec((tm, HEAD_DIM), lambda t: (tile_in(t) % tiles_per_seq, 0)),
            const((N_HEADS, chunk, chunk)),
            const((N_HEADS, chunk, HEAD_DIM)),
            const((N_HEADS, chunk, HEAD_DIM)),
            const((N_HEADS, F32_SUBLANES, HEAD_DIM)),
            per_layer((CONV_K, CONV_WIDTH)),
            per_layer((1, RET_WIDTH)),
            per_layer((D_MODEL, D_MODEL)),
            per_layer((1, D_MODEL)),
            per_layer((D_MODEL, D_FF)),
            per_layer((D_FF, D_MODEL)),
            const((1, D_MODEL)),
        ] + extra_specs,
        out_specs=[pl.BlockSpec((tm, D_MODEL), lambda t: (tile_out(t), 0))] + state_specs,
        out_shape=[
            jax.ShapeDtypeStruct((rows, D_MODEL), F32),
            jax.ShapeDtypeStruct((depth, n_seq, N_HEADS, HEAD_DIM, HEAD_DIM), F32),
            jax.ShapeDtypeStruct((depth, n_seq, CONV_K - 1, CONV_WIDTH), F32),
        ],
        input_output_aliases=aliases,
        scratch_shapes=[
            pltpu.VMEM((tm, D_MODEL), BF16),
            pltpu.VMEM((F32_SUBLANES + chunk, CONV_WIDTH), F32),
            pltpu.VMEM((2, tm, D_MODEL), F32),
        ] + ([pltpu.VMEM((tm, D_MODEL), F32)] if final_norm else []),
        compiler_params=pltpu.CompilerParams(
            dimension_semantics=("arbitrary",),
            vmem_limit_bytes=V7X_VMEM_LIMIT_BYTES),
        name="prompt_layer",
    )(x2d, params["ln1"], params["w_in"], cos2, sin2, dmat, qdec, kdec, cdec, params["conv_w"],
      params["ret_w"], params["w_out"], params["ln2"], params["w_up"], params["w_down"], params["ln_f"],
      *prev_states)


def _sample_mixer_kernel(x_ref, ln1_ref, win_ref, cos_ref, sin_ref, dmat_ref, qdec_ref, kdec_ref,
                         cdec_ref, convw_ref, retw_ref, wout_ref, r0_ref, cb0_ref, *rest, nb, tm,
                         state_layer):
    xo_ref, rout_ref, cout_ref, mix_ref, ubuf_ref = rest[-5:]
    mixer_refs = MixerRefs(ln1_ref, win_ref, cos_ref, sin_ref, dmat_ref, qdec_ref, kdec_ref, cdec_ref,
                           convw_ref, retw_ref, wout_ref)
    if state_layer is not None:
        rout_ref[...] = jnp.zeros(rout_ref.shape, F32)
        cout_ref[...] = jnp.zeros(cout_ref.shape, F32)
    rnew_ref, cnew_ref = _own_layer_state(rout_ref, cout_ref, state_layer)
    rnew_ref[...] = r0_ref[...]
    cnew_ref[...] = cb0_ref[...]

    def store_out(r0, r1, y):
        xo_ref[r0:r1, :] = y

    _run_all(_mixer_stages(lambda r0, r1: x_ref[r0:r1, :], store_out, mixer_refs, rnew_ref, cnew_ref,
                           mix_ref, ubuf_ref, nb=nb, tm=tm, chunk=tm, n_groups=1))


def _sample_mixer_call(x2d, r_all, c_all, layer, depth, params, prev_states, *, nb, tm):
    assert tm % 16 == 0 and x2d.shape[0] == nb * tm
    state_blocks = {"ret": (nb, N_HEADS, HEAD_DIM, HEAD_DIM), "conv": (nb, CONV_K - 1, CONV_WIDTH)}
    extra_specs, state_specs, aliases, state_layer = _stacked_state_plumbing(
        prev_states, 14, layer, depth, state_blocks.get, lambda kind, i: (0,) * len(state_blocks[kind]))
    cos2, sin2 = _rope_tables(PAST_LEN, tm)
    dmat, qdec, kdec, cdec = _decay_tables(tm)
    const = lambda shape: pl.BlockSpec(shape, lambda i: (0,) * len(shape))
    per_layer = lambda shape: pl.BlockSpec((None,) + shape, lambda i: (layer,) + (0,) * len(shape))
    return pl.pallas_call(
        functools.partial(_sample_mixer_kernel, nb=nb, tm=tm, state_layer=state_layer),
        grid=(1,),
        in_specs=[
            const((nb * tm, D_MODEL)),
            per_layer((1, D_MODEL)),
            per_layer((D_MODEL, IN_COLS)),
            const((tm, HEAD_DIM)),
            const((tm, HEAD_DIM)),
            const((N_HEADS, tm, tm)),
            const((N_HEADS, tm, HEAD_DIM)),
            const((N_HEADS, tm, HEAD_DIM)),
            const((N_HEADS, F32_SUBLANES, HEAD_DIM)),
            per_layer((CONV_K, CONV_WIDTH)),
            per_layer((1, RET_WIDTH)),
            per_layer((D_MODEL, D_MODEL)),
            per_layer((nb, N_HEADS, HEAD_DIM, HEAD_DIM)),
            per_layer((nb, CONV_K - 1, CONV_WIDTH)),
        ] + extra_specs,
        out_specs=[const((nb * tm, D_MODEL))] + state_specs,
        out_shape=[
            jax.ShapeDtypeStruct((nb * tm, D_MODEL), F32),
            jax.ShapeDtypeStruct((depth, nb, N_HEADS, HEAD_DIM, HEAD_DIM), F32),
            jax.ShapeDtypeStruct((depth, nb, CONV_K - 1, CONV_WIDTH), F32),
        ],
        input_output_aliases=aliases,
        scratch_shapes=[
            pltpu.VMEM((nb * tm, D_MODEL), BF16),
            pltpu.VMEM((F32_SUBLANES + tm, CONV_WIDTH), F32),
        ],
        compiler_params=pltpu.CompilerParams(
            dimension_semantics=("arbitrary",),
            vmem_limit_bytes=V7X_VMEM_LIMIT_BYTES),
        name="sample_mixer",
    )(x2d, params["ln1"], params["w_in"], cos2, sin2, dmat, qdec, kdec, cdec, params["conv_w"],
      params["ret_w"], params["w_out"], r_all, c_all, *prev_states)


def _sample_mlp_kernel(x_ref, ln2_ref, wup_ref, wdown_ref, lnf_ref, o_ref, *, final_norm):
    def store_out(r0, r1, y):
        o_ref[r0:r1, :] = y

    rows = x_ref.shape[0]
    _run_all(_mlp_stages(lambda r0, r1: x_ref[r0:r1, :], store_out,
                         MlpRefs(ln2_ref, wup_ref, wdown_ref, lnf_ref), rows=rows, n_groups=1))
    if final_norm:
        _run_all(_final_norm_stages(lambda r0, r1: o_ref[r0:r1, :], store_out, lnf_ref,
                                    rows=rows, n_groups=1))


def _sample_mlp_call(x2d, layer, params, *, final_norm):
    t, _ = x2d.shape
    const = lambda shape: pl.BlockSpec(shape, lambda i: (0,) * len(shape))
    per_layer = lambda shape: pl.BlockSpec((None,) + shape, lambda i: (layer,) + (0,) * len(shape))
    return pl.pallas_call(
        functools.partial(_sample_mlp_kernel, final_norm=final_norm),
        grid=(1,),
        in_specs=[
            const((t, D_MODEL)),
            per_layer((1, D_MODEL)),
            per_layer((D_MODEL, D_FF)),
            per_layer((D_FF, D_MODEL)),
            const((1, D_MODEL)),
        ],
        out_specs=const((t, D_MODEL)),
        out_shape=jax.ShapeDtypeStruct((t, D_MODEL), F32),
        compiler_params=pltpu.CompilerParams(
            dimension_semantics=("arbitrary",),
            vmem_limit_bytes=V7X_VMEM_LIMIT_BYTES),
        name="sample_mlp",
    )(x2d, params["ln2"], params["w_up"], params["w_down"], params["ln_f"])


def kernel(x_prompt, x_sample, state_ret, state_conv, ln1_w, w_in, conv_w, ret_norm_w, w_out, ln2_w, w_mlp_up, w_mlp_down, ln_f_w):
    depth = w_in.shape[0]
    b_p, l_p, _ = x_prompt.shape
    b_s, l_s, _ = x_sample.shape

    params = dict(
        ln1=ln1_w.reshape(depth, 1, D_MODEL),
        w_in=w_in.astype(BF16),
        conv_w=jnp.swapaxes(conv_w, 1, 2),
        ret_w=ret_norm_w.reshape(depth, 1, RET_WIDTH),
        w_out=w_out.astype(BF16),
        ln2=ln2_w.reshape(depth, 1, D_MODEL),
        w_up=w_mlp_up.astype(BF16),
        w_down=w_mlp_down.astype(BF16),
        ln_f=ln_f_w.reshape(1, D_MODEL),
    )

    xp = x_prompt.reshape(b_p * l_p, D_MODEL)
    xs = x_sample.reshape(b_s * l_s, D_MODEL)
    states_p, states_s = (), ()
    for i in range(depth):
        last = i == depth - 1
        xp, *states_p = _prompt_layer_call(xp, l_p, i, depth, params, states_p, final_norm=last)
        xs, *states_s = _sample_mixer_call(xs, state_ret, state_conv, i, depth, params, states_s,
                                           nb=b_s, tm=l_s)
        xs = _sample_mlp_call(xs, i, params, final_norm=last)

    return (xp.reshape(b_p, l_p, D_MODEL), xs.reshape(b_s, l_s, D_MODEL),
            states_p[0], states_p[1], states_s[0], states_s[1])
```

```python
import functools
from typing import Any, NamedTuple

import numpy as np
import jax
import jax.numpy as jnp
from jax import lax
from jax.experimental import pallas as pl
from jax.experimental.pallas import tpu as pltpu

D_MODEL = 1024
N_HEADS = 4
HEAD_DIM = 128
RET_WIDTH = N_HEADS * HEAD_DIM
CONV_WIDTH = D_MODEL - RET_WIDTH
CONV_K = 3
D_FF = 4 * D_MODEL
IN_COLS = 4 * RET_WIDTH + 3 * CONV_WIDTH
PAST_LEN = 2048
ROPE_BASE = 10000.0
RMS_EPS = 1e-6
GN_EPS = 1e-5

V7X_VMEM_LIMIT_BYTES = 62 * 1024 * 1024
F32_SUBLANES = 8
PROMPT_TILE = 512
PROMPT_CHUNK = 256

BF16 = jnp.bfloat16
F32 = jnp.float32


def _rope_tables(pos0, length):
    half = HEAD_DIM // 2
    inv = ROPE_BASE ** (-np.arange(half, dtype=np.float64) / half)
    ang = (pos0 + np.arange(length, dtype=np.float64))[:, None] * inv[None, :]
    cos, sin = np.cos(ang), np.sin(ang)
    cos2 = np.concatenate([cos, cos], axis=-1)
    sin2 = np.concatenate([-sin, sin], axis=-1)
    return jnp.asarray(cos2, F32), jnp.asarray(sin2, F32)


def _decay_tables(chunk):
    lg = np.log(1.0 - np.exp2(-5.0 - np.arange(N_HEADS, dtype=np.float64)))
    idx = np.arange(chunk, dtype=np.float64)
    diff = idx[:, None] - idx[None, :]
    scale = HEAD_DIM ** -0.5
    dmat = np.where(diff >= 0, np.exp(np.maximum(diff, 0.0)[None] * lg[:, None, None]), 0.0) * scale
    qdec = np.exp((idx[None, :] + 1.0) * lg[:, None])
    kdec = np.exp((chunk - 1.0 - idx[None, :]) * lg[:, None]) * scale
    cdec = np.exp(chunk * lg)
    lanes = np.ones((1, 1, HEAD_DIM))
    return (jnp.asarray(dmat, F32),
            jnp.asarray(qdec[:, :, None] * lanes, F32),
            jnp.asarray(kdec[:, :, None] * lanes, F32),
            jnp.asarray(cdec[:, None, None] * np.ones((1, F32_SUBLANES, HEAD_DIM)), F32))


def _rms_norm(x, w):
    return x * lax.rsqrt(jnp.mean(x * x, axis=-1, keepdims=True) + RMS_EPS) * w


class MixerRefs(NamedTuple):
    ln1: Any
    w_in: Any
    cos: Any
    sin: Any
    dmat: Any
    qdec: Any
    kdec: Any
    cdec: Any
    conv_w: Any
    ret_w: Any
    w_out: Any


class MlpRefs(NamedTuple):
    ln2: Any
    w_up: Any
    w_down: Any
    ln_f: Any


def _rope(t, cos, sin):
    return t * cos + pltpu.roll(t, HEAD_DIM // 2, axis=1) * sin


def _retention_stages(proj, lo_row, out_row, s, c, p, rstate_ref, mix_ref, *, chunk):
    cos = p.cos[c * chunk:(c + 1) * chunk, :]
    sin = p.sin[c * chunk:(c + 1) * chunk, :]
    heads = range(N_HEADS)
    col = lambda part, h: proj[lo_row:lo_row + chunk, part * RET_WIDTH + h * HEAD_DIM:
                               part * RET_WIDTH + (h + 1) * HEAD_DIM]
    qs = [_rope(col(0, h), cos, sin) for h in heads]
    ks = [_rope(col(1, h), cos, sin) for h in heads]
    vs = [col(2, h).astype(BF16) for h in heads]
    scs = [lax.dot_general(qs[h].astype(BF16), ks[h].astype(BF16), (((1,), (1,)), ((), ())),
                           preferred_element_type=F32) for h in heads]
    yield
    states = [rstate_ref[s, h] for h in heads]
    os = [jnp.dot((scs[h] * p.dmat[h]).astype(BF16), vs[h], preferred_element_type=F32)
          + jnp.dot((qs[h] * p.qdec[h]).astype(BF16), states[h].astype(BF16),
                    preferred_element_type=F32) for h in heads]
    for h in heads:
        kd = (ks[h] * p.kdec[h]).astype(BF16)
        rstate_ref[s, h] = (states[h] * p.cdec[h, 0:1, :]
                            + lax.dot_general(kd, vs[h], (((0,), (0,)), ((), ())),
                                              preferred_element_type=F32))
    yield
    for h in heads:
        lo = h * HEAD_DIM
        o, g = os[h], col(3, h)
        oc = o - jnp.mean(o, axis=-1, keepdims=True)
        on = oc * lax.rsqrt(jnp.mean(oc * oc, axis=-1, keepdims=True) + GN_EPS)
        ret = g * jax.nn.sigmoid(g) * (on * p.ret_w[:, lo:lo + HEAD_DIM])
        mix_ref[out_row:out_row + chunk, lo:lo + HEAD_DIM] = ret.astype(BF16)


def _conv_block(proj, lo_row, out_row, s, p, cstate_ref, mix_ref, ubuf_ref, *, chunk):
    base = 4 * RET_WIDTH
    gate_b = proj[lo_row:lo_row + chunk, base:base + CONV_WIDTH]
    gate_c = proj[lo_row:lo_row + chunk, base + CONV_WIDTH:base + 2 * CONV_WIDTH]
    hc = proj[lo_row:lo_row + chunk, base + 2 * CONV_WIDTH:base + 3 * CONV_WIDTH]
    u = gate_c * hc
    pad = F32_SUBLANES
    ubuf_ref[pad - (CONV_K - 1):pad, :] = cstate_ref[s]
    ubuf_ref[pad:pad + chunk, :] = u
    z = (ubuf_ref[pad - 2:pad - 2 + chunk, :] * p.conv_w[0:1, :]
         + ubuf_ref[pad - 1:pad - 1 + chunk, :] * p.conv_w[1:2, :]
         + u * p.conv_w[2:3, :])
    cstate_ref[s] = ubuf_ref[pad + chunk - (CONV_K - 1):pad + chunk, :]
    mix_ref[out_row:out_row + chunk, RET_WIDTH:] = (gate_b * z).astype(BF16)


def _mixer_input(x, ln1_ref):
    return _rms_norm(x, ln1_ref[...]).astype(BF16)


def _mixer_stages(load_x, load_n, store_y, p, rstate_ref, cstate_ref, mix_ref, ubuf_ref,
                  *, nb, tm, chunk, n_groups):
    rows = nb * tm
    g_rows = rows // n_groups
    assert rows % n_groups == 0 and g_rows % chunk == 0 and tm % chunk == 0
    projs = []
    for g in range(n_groups):
        n = load_n(g * g_rows, (g + 1) * g_rows)
        projs.append(jnp.dot(n, p.w_in[...], preferred_element_type=F32))
        yield
    for g in range(n_groups):
        r0 = g * g_rows
        for lo_row in range(0, g_rows, chunk):
            s, c = divmod(r0 + lo_row, tm)
            c //= chunk
            yield from _retention_stages(projs[g], lo_row, r0 + lo_row, s, c, p, rstate_ref, mix_ref,
                                         chunk=chunk)
            _conv_block(projs[g], lo_row, r0 + lo_row, s, p, cstate_ref, mix_ref, ubuf_ref, chunk=chunk)
        yield
        store_y(r0, r0 + g_rows, load_x(r0, r0 + g_rows)
                + jnp.dot(mix_ref[r0:r0 + g_rows, :], p.w_out[...], preferred_element_type=F32))
        if g + 1 < n_groups:
            yield


def _mlp_input(x, ln2_ref):
    return _rms_norm(x, ln2_ref[...]).astype(BF16)


def _mlp_stages(load_x, load_h2, store_y, p, act_ref, *, rows, n_groups, n_col_pieces):
    g_rows = rows // n_groups
    assert rows % n_groups == 0
    for g in range(n_groups):
        r0 = g * g_rows
        up = jnp.dot(load_h2(r0, r0 + g_rows), p.w_up[...], preferred_element_type=F32)
        act_ref[r0:r0 + g_rows, :] = jnp.square(jnp.maximum(up, 0.0)).astype(BF16)
        yield
    cols = D_MODEL // n_col_pieces
    for g in range(n_groups):
        r0 = g * g_rows
        for c0 in range(0, D_MODEL, cols):
            store_y(r0, r0 + g_rows, c0, c0 + cols, load_x(r0, r0 + g_rows, c0, c0 + cols)
                    + jnp.dot(act_ref[r0:r0 + g_rows, :], p.w_down[:, c0:c0 + cols],
                              preferred_element_type=F32))
            if (g + 1, c0 + cols) != (n_groups, D_MODEL):
                yield


def _final_norm_stages(load_y, store_y, lnf_ref, *, rows, n_groups):
    g_rows = rows // n_groups
    for g in range(n_groups):
        r0 = g * g_rows
        store_y(r0, r0 + g_rows, _rms_norm(load_y(r0, r0 + g_rows), lnf_ref[...]))
        if g + 1 < n_groups:
            yield


def _run_stages(order, **stages):
    for name in order:
        next(stages[name], None)
    for gen in stages.values():
        assert next(gen, "done") == "done"


def _run_all(stages):
    for _ in stages:
        pass


DOWN_COL_PIECES = 4
FUSED_STAGE_ORDER = "LMLM" + 2 * "LMLMLMLM"
FUSED_STAGE_ORDER_FINAL_NORM = "LMLM" + 2 * "NLMLMLMLM"
MLP_NORM_STAGE_ORDER = "LL" + 2 * "NLLLL"


def _prompt_layer_kernel(x_ref, ln1_ref, win_ref, cos_ref, sin_ref, dmat_ref, qdec_ref, kdec_ref,
                         cdec_ref, convw_ref, retw_ref, wout_ref, ln2_ref, wup_ref, wdown_ref, lnf_ref,
                         *rest, n_tiles, tiles_per_seq, tm, chunk, final_norm, state_layer):
    if final_norm:
        xo_ref, rout_ref, cout_ref, mix_ref, ubuf_ref, x1_ref, h2a_ref, act_ref, yraw_ref = rest[-9:]
    else:
        xo_ref, rout_ref, cout_ref, mix_ref, ubuf_ref, x1_ref, h2a_ref, act_ref = rest[-8:]
    t = pl.program_id(0)
    mixer_refs = MixerRefs(ln1_ref, win_ref, cos_ref, sin_ref, dmat_ref, qdec_ref, kdec_ref, cdec_ref,
                           convw_ref, retw_ref, wout_ref)
    mlp_refs = MlpRefs(ln2_ref, wup_ref, wdown_ref, lnf_ref)

    @pl.when(jnp.logical_and(lax.rem(t, tiles_per_seq) == 0, t < n_tiles))
    def _():
        rout_ref[...] = jnp.zeros(rout_ref.shape, F32)
        cout_ref[...] = jnp.zeros(cout_ref.shape, F32)

    rnew_ref, cnew_ref = _own_layer_state(rout_ref, cout_ref, state_layer)

    cur = lax.rem(t, 2)
    prev = 1 - cur

    def store_x1(r0, r1, y):
        x1_ref[cur, r0:r1, :] = y
        if r0 == 0:
            h2a_ref[...] = _mlp_input(y, ln2_ref)

    def store_out(r0, r1, c0, c1, y):
        xo_ref[r0:r1, c0:c1] = y

    def store_raw(r0, r1, c0, c1, y):
        yraw_ref[r0:r1, c0:c1] = y

    n_groups = tm // chunk

    def mixer():
        return _mixer_stages(lambda r0, r1: x_ref[r0:r1, :],
                             lambda r0, r1: _mixer_input(x_ref[r0:r1, :], ln1_ref), store_x1,
                             mixer_refs, rnew_ref, cnew_ref, mix_ref, ubuf_ref,
                             nb=1, tm=tm, chunk=chunk, n_groups=n_groups)

    def load_h2(r0, r1):
        assert r1 - r0 == chunk
        return h2a_ref[...] if r0 == 0 else _mlp_input(x1_ref[prev, r0:r1, :], ln2_ref)

    def mlp():
        return _mlp_stages(lambda r0, r1, c0, c1: x1_ref[prev, r0:r1, c0:c1], load_h2,
                           store_raw if final_norm else store_out, mlp_refs, act_ref,
                           rows=tm, n_groups=n_groups, n_col_pieces=DOWN_COL_PIECES)

    def norm():
        return _final_norm_stages(lambda r0, r1: yraw_ref[r0:r1, :],
                                  lambda r0, r1, y: store_out(r0, r1, 0, D_MODEL, y), lnf_ref,
                                  rows=tm, n_groups=n_groups)

    @pl.when(t == 0)
    def _():
        if final_norm:
            yraw_ref[...] = jnp.zeros(yraw_ref.shape, F32)
        _run_all(mixer())

    if not final_norm:
        @pl.when(jnp.logical_and(t > 0, t < n_tiles))
        def _():
            _run_stages(FUSED_STAGE_ORDER, M=mixer(), L=mlp())

        @pl.when(t == n_tiles)
        def _():
            _run_all(mlp())
    else:
        @pl.when(jnp.logical_and(t > 0, t < n_tiles))
        def _():
            _run_stages(FUSED_STAGE_ORDER_FINAL_NORM, M=mixer(), L=mlp(), N=norm())

        @pl.when(t == n_tiles)
        def _():
            _run_stages(MLP_NORM_STAGE_ORDER, L=mlp(), N=norm())

        @pl.when(t == n_tiles + 1)
        def _():
            _run_all(norm())


def _stacked_state_plumbing(prev_states, n_fixed_inputs, layer, depth, state_block, state_index):
    if not prev_states:
        assert layer == 0
        out_specs = [pl.BlockSpec((depth,) + state_block(k), lambda i, k=k: (0,) + state_index(k, i))
                     for k in ("ret", "conv")]
        return [], out_specs, {}, layer
    out_specs = [pl.BlockSpec((None,) + state_block(k), lambda i, k=k: (layer,) + state_index(k, i))
                 for k in ("ret", "conv")]
    extra_specs = [pl.BlockSpec(memory_space=pl.ANY) for _ in prev_states]
    aliases = {n_fixed_inputs + j: 1 + j for j in range(len(prev_states))}
    return extra_specs, out_specs, aliases, None


def _own_layer_state(rout_ref, cout_ref, state_layer):
    if state_layer is None:
        return rout_ref, cout_ref
    return rout_ref.at[state_layer], cout_ref.at[state_layer]


def _prompt_layer_call(x2d, seq_len, layer, depth, params, prev_states, *, final_norm):
    rows, _ = x2d.shape
    tm, chunk = PROMPT_TILE, PROMPT_CHUNK
    assert tm == 2 * chunk, "the stage orders are written for two row groups per tile"
    assert seq_len % tm == 0 and rows % seq_len == 0 and chunk % 16 == 0
    tiles_per_seq = seq_len // tm
    n_tiles = rows // tm
    n_seq = rows // seq_len
    cos2, sin2 = _rope_tables(0, seq_len)
    dmat, qdec, kdec, cdec = _decay_tables(chunk)

    def tile_in(t):
        return jnp.minimum(t, n_tiles - 1)

    out_lag = 2 if final_norm else 1

    def tile_out(t):
        return jnp.clip(t - out_lag, 0, n_tiles - 1)

    const = lambda shape: pl.BlockSpec(shape, lambda t: (0,) * len(shape))
    per_layer = lambda shape: pl.BlockSpec((None,) + shape, lambda t: (layer,) + (0,) * len(shape))
    state_blocks = {"ret": (1, N_HEADS, HEAD_DIM, HEAD_DIM), "conv": (1, CONV_K - 1, CONV_WIDTH)}
    extra_specs, state_specs, aliases, state_layer = _stacked_state_plumbing(
        prev_states, 16, layer, depth, state_blocks.get,
        lambda kind, t: (tile_in(t) // tiles_per_seq,) + (0,) * (len(state_blocks[kind]) - 1))
    kernel = functools.partial(_prompt_layer_kernel, n_tiles=n_tiles, tiles_per_seq=tiles_per_seq,
                               tm=tm, chunk=chunk, final_norm=final_norm, state_layer=state_layer)
    return pl.pallas_call(
        kernel,
        grid=(n_tiles + out_lag,),
        in_specs=[
            pl.BlockSpec((tm, D_MODEL), lambda t: (tile_in(t), 0)),
            per_layer((1, D_MODEL)),
            per_layer((D_MODEL, IN_COLS)),
            pl.BlockSpec((tm, HEAD_DIM), lambda t: (tile_in(t) % tiles_per_seq, 0)),
            pl.BlockSpec((tm, HEAD_DIM), lambda t: (tile_in(t) % tiles_per_seq, 0)),
            const((N_HEADS, chunk, chunk)),
            const((N_HEADS, chunk, HEAD_DIM)),
            const((N_HEADS, chunk, HEAD_DIM)),
            const((N_HEADS, F32_SUBLANES, HEAD_DIM)),
            per_layer((CONV_K, CONV_WIDTH)),
            per_layer((1, RET_WIDTH)),
            per_layer((D_MODEL, D_MODEL)),
            per_layer((1, D_MODEL)),
            per_layer((D_MODEL, D_FF)),
            per_layer((D_FF, D_MODEL)),
            const((1, D_MODEL)),
        ] + extra_specs,
        out_specs=[pl.BlockSpec((tm, D_MODEL), lambda t: (tile_out(t), 0))] + state_specs,
        out_shape=[
            jax.ShapeDtypeStruct((rows, D_MODEL), F32),
            jax.ShapeDtypeStruct((depth, n_seq, N_HEADS, HEAD_DIM, HEAD_DIM), F32),
            jax.ShapeDtypeStruct((depth, n_seq, CONV_K - 1, CONV_WIDTH), F32),
        ],
        input_output_aliases=aliases,
        scratch_shapes=[
            pltpu.VMEM((tm, D_MODEL), BF16),
            pltpu.VMEM((F32_SUBLANES + chunk, CONV_WIDTH), F32),
            pltpu.VMEM((2, tm, D_MODEL), F32),
            pltpu.VMEM((chunk, D_MODEL), BF16),
            pltpu.VMEM((tm, D_FF), BF16),
        ] + ([pltpu.VMEM((tm, D_MODEL), F32)] if final_norm else []),
        compiler_params=pltpu.CompilerParams(
            dimension_semantics=("arbitrary",),
            vmem_limit_bytes=V7X_VMEM_LIMIT_BYTES),
        name="prompt_layer",
    )(x2d, params["ln1"], params["w_in"], cos2, sin2, dmat, qdec, kdec, cdec, params["conv_w"],
      params["ret_w"], params["w_out"], params["ln2"], params["w_up"], params["w_down"], params["ln_f"],
      *prev_states)


def _sample_mixer_kernel(x_ref, ln1_ref, win_ref, cos_ref, sin_ref, dmat_ref, qdec_ref, kdec_ref,
                         cdec_ref, convw_ref, retw_ref, wout_ref, r0_ref, cb0_ref, *rest, nb, tm,
                         state_layer):
    xo_ref, rout_ref, cout_ref, mix_ref, ubuf_ref = rest[-5:]
    mixer_refs = MixerRefs(ln1_ref, win_ref, cos_ref, sin_ref, dmat_ref, qdec_ref, kdec_ref, cdec_ref,
                           convw_ref, retw_ref, wout_ref)
    if state_layer is not None:
        rout_ref[...] = jnp.zeros(rout_ref.shape, F32)
        cout_ref[...] = jnp.zeros(cout_ref.shape, F32)
    rnew_ref, cnew_ref = _own_layer_state(rout_ref, cout_ref, state_layer)
    rnew_ref[...] = r0_ref[...]
    cnew_ref[...] = cb0_ref[...]

    def store_out(r0, r1, y):
        xo_ref[r0:r1, :] = y

    _run_all(_mixer_stages(lambda r0, r1: x_ref[r0:r1, :],
                           lambda r0, r1: _mixer_input(x_ref[r0:r1, :], ln1_ref), store_out,
                           mixer_refs, rnew_ref, cnew_ref, mix_ref, ubuf_ref,
                           nb=nb, tm=tm, chunk=tm, n_groups=1))


def _sample_mixer_call(x2d, r_all, c_all, layer, depth, params, prev_states, *, nb, tm):
    assert tm % 16 == 0 and x2d.shape[0] == nb * tm
    state_blocks = {"ret": (nb, N_HEADS, HEAD_DIM, HEAD_DIM), "conv": (nb, CONV_K - 1, CONV_WIDTH)}
    extra_specs, state_specs, aliases, state_layer = _stacked_state_plumbing(
        prev_states, 14, layer, depth, state_blocks.get, lambda kind, i: (0,) * len(state_blocks[kind]))
    cos2, sin2 = _rope_tables(PAST_LEN, tm)
    dmat, qdec, kdec, cdec = _decay_tables(tm)
    const = lambda shape: pl.BlockSpec(shape, lambda i: (0,) * len(shape))
    per_layer = lambda shape: pl.BlockSpec((None,) + shape, lambda i: (layer,) + (0,) * len(shape))
    return pl.pallas_call(
        functools.partial(_sample_mixer_kernel, nb=nb, tm=tm, state_layer=state_layer),
        grid=(1,),
        in_specs=[
            const((nb * tm, D_MODEL)),
            per_layer((1, D_MODEL)),
            per_layer((D_MODEL, IN_COLS)),
            const((tm, HEAD_DIM)),
            const((tm, HEAD_DIM)),
            const((N_HEADS, tm, tm)),
            const((N_HEADS, tm, HEAD_DIM)),
            const((N_HEADS, tm, HEAD_DIM)),
            const((N_HEADS, F32_SUBLANES, HEAD_DIM)),
            per_layer((CONV_K, CONV_WIDTH)),
            per_layer((1, RET_WIDTH)),
            per_layer((D_MODEL, D_MODEL)),
            per_layer((nb, N_HEADS, HEAD_DIM, HEAD_DIM)),
            per_layer((nb, CONV_K - 1, CONV_WIDTH)),
        ] + extra_specs,
        out_specs=[const((nb * tm, D_MODEL))] + state_specs,
        out_shape=[
            jax.ShapeDtypeStruct((nb * tm, D_MODEL), F32),
            jax.ShapeDtypeStruct((depth, nb, N_HEADS, HEAD_DIM, HEAD_DIM), F32),
            jax.ShapeDtypeStruct((depth, nb, CONV_K - 1, CONV_WIDTH), F32),
        ],
        input_output_aliases=aliases,
        scratch_shapes=[
            pltpu.VMEM((nb * tm, D_MODEL), BF16),
            pltpu.VMEM((F32_SUBLANES + tm, CONV_WIDTH), F32),
        ],
        compiler_params=pltpu.CompilerParams(
            dimension_semantics=("arbitrary",),
            vmem_limit_bytes=V7X_VMEM_LIMIT_BYTES),
        name="sample_mixer",
    )(x2d, params["ln1"], params["w_in"], cos2, sin2, dmat, qdec, kdec, cdec, params["conv_w"],
      params["ret_w"], params["w_out"], r_all, c_all, *prev_states)


def _sample_mlp_kernel(x_ref, ln2_ref, wup_ref, wdown_ref, lnf_ref, o_ref, act_ref, *, final_norm):
    def store_out(r0, r1, c0, c1, y):
        o_ref[r0:r1, c0:c1] = y

    rows = x_ref.shape[0]
    _run_all(_mlp_stages(lambda r0, r1, c0, c1: x_ref[r0:r1, c0:c1],
                         lambda r0, r1: _mlp_input(x_ref[r0:r1, :], ln2_ref), store_out,
                         MlpRefs(ln2_ref, wup_ref, wdown_ref, lnf_ref), act_ref,
                         rows=rows, n_groups=1, n_col_pieces=1))
    if final_norm:
        _run_all(_final_norm_stages(lambda r0, r1: o_ref[r0:r1, :],
                                    lambda r0, r1, y: store_out(r0, r1, 0, D_MODEL, y), lnf_ref,
                                    rows=rows, n_groups=1))


def _sample_mlp_call(x2d, layer, params, *, final_norm):
    t, _ = x2d.shape
    const = lambda shape: pl.BlockSpec(shape, lambda i: (0,) * len(shape))
    per_layer = lambda shape: pl.BlockSpec((None,) + shape, lambda i: (layer,) + (0,) * len(shape))
    return pl.pallas_call(
        functools.partial(_sample_mlp_kernel, final_norm=final_norm),
        grid=(1,),
        in_specs=[
            const((t, D_MODEL)),
            per_layer((1, D_MODEL)),
            per_layer((D_MODEL, D_FF)),
            per_layer((D_FF, D_MODEL)),
            const((1, D_MODEL)),
        ],
        out_specs=const((t, D_MODEL)),
        out_shape=jax.ShapeDtypeStruct((t, D_MODEL), F32),
        scratch_shapes=[pltpu.VMEM((t, D_FF), BF16)],
        compiler_params=pltpu.CompilerParams(
            dimension_semantics=("arbitrary",),
            vmem_limit_bytes=V7X_VMEM_LIMIT_BYTES),
        name="sample_mlp",
    )(x2d, params["ln2"], params["w_up"], params["w_down"], params["ln_f"])


def kernel(x_prompt, x_sample, state_ret, state_conv, ln1_w, w_in, conv_w, ret_norm_w, w_out, ln2_w, w_mlp_up, w_mlp_down, ln_f_w):
    depth = w_in.shape[0]
    b_p, l_p, _ = x_prompt.shape
    b_s, l_s, _ = x_sample.shape

    params = dict(
        ln1=ln1_w.reshape(depth, 1, D_MODEL),
        w_in=w_in.astype(BF16),
        conv_w=jnp.swapaxes(conv_w, 1, 2),
        ret_w=ret_norm_w.reshape(depth, 1, RET_WIDTH),
        w_out=w_out.astype(BF16),
        ln2=ln2_w.reshape(depth, 1, D_MODEL),
        w_up=w_mlp_up.astype(BF16),
        w_down=w_mlp_down.astype(BF16),
        ln_f=ln_f_w.reshape(1, D_MODEL),
    )

    xp = x_prompt.reshape(b_p * l_p, D_MODEL)
    xs = x_sample.reshape(b_s * l_s, D_MODEL)
    states_p, states_s = (), ()
    for i in range(depth):
        last = i == depth - 1
        xp, *states_p = _prompt_layer_call(xp, l_p, i, depth, params, states_p, final_norm=last)
        xs, *states_s = _sample_mixer_call(xs, state_ret, state_conv, i, depth, params, states_s,
                                           nb=b_s, tm=l_s)
        xs = _sample_mlp_call(xs, i, params, final_norm=last)

    return (xp.reshape(b_p, l_p, D_MODEL), xs.reshape(b_s, l_s, D_MODEL),
            states_p[0], states_p[1], states_s[0], states_s[1])
```

```python
import functools
from typing import Any, NamedTuple

import numpy as np
import jax
import jax.numpy as jnp
from jax import lax
from jax.experimental import pallas as pl
from jax.experimental.pallas import tpu as pltpu

D_MODEL = 1024
N_HEADS = 4
HEAD_DIM = 128
RET_WIDTH = N_HEADS * HEAD_DIM
CONV_WIDTH = D_MODEL - RET_WIDTH
CONV_K = 3
D_FF = 4 * D_MODEL
IN_COLS = 4 * RET_WIDTH + 3 * CONV_WIDTH
PAST_LEN = 2048
ROPE_BASE = 10000.0
RMS_EPS = 1e-6
GN_EPS = 1e-5

V7X_VMEM_LIMIT_BYTES = 60 * 1024 * 1024
F32_SUBLANES = 8
PROMPT_TILE = 512
PROMPT_CHUNK = 256
SAMPLE_FF_CHUNK = 1024

BF16 = jnp.bfloat16
F32 = jnp.float32


def _rope_tables(pos0, length):
    half = HEAD_DIM // 2
    inv = ROPE_BASE ** (-np.arange(half, dtype=np.float64) / half)
    ang = (pos0 + np.arange(length, dtype=np.float64))[:, None] * inv[None, :]
    cos, sin = np.cos(ang), np.sin(ang)
    cos2 = np.concatenate([cos, cos], axis=-1)
    sin2 = np.concatenate([-sin, sin], axis=-1)
    return jnp.asarray(cos2, F32), jnp.asarray(sin2, F32)


def _decay_tables(chunk):
    lg = np.log(1.0 - np.exp2(-5.0 - np.arange(N_HEADS, dtype=np.float64)))
    idx = np.arange(chunk, dtype=np.float64)
    diff = idx[:, None] - idx[None, :]
    scale = HEAD_DIM ** -0.5
    dmat = np.where(diff >= 0, np.exp(np.maximum(diff, 0.0)[None] * lg[:, None, None]), 0.0) * scale
    qdec = np.exp((idx[None, :] + 1.0) * lg[:, None])
    kdec = np.exp((chunk - 1.0 - idx[None, :]) * lg[:, None]) * scale
    cdec = np.exp(chunk * lg)
    lanes = np.ones((1, 1, HEAD_DIM))
    return (jnp.asarray(dmat, F32),
            jnp.asarray(qdec[:, :, None] * lanes, F32),
            jnp.asarray(kdec[:, :, None] * lanes, F32),
            jnp.asarray(cdec[:, None, None] * np.ones((1, F32_SUBLANES, HEAD_DIM)), F32))


def _rms_norm(x, w):
    return x * lax.rsqrt(jnp.mean(x * x, axis=-1, keepdims=True) + RMS_EPS) * w


class MixerRefs(NamedTuple):
    ln1: Any
    w_in: Any
    cos: Any
    sin: Any
    dmat: Any
    qdec: Any
    kdec: Any
    cdec: Any
    conv_w: Any
    ret_w: Any
    w_out: Any


class MlpRefs(NamedTuple):
    ln2: Any
    w_up: Any
    w_down: Any
    ln_f: Any


def _rope(t, cos, sin):
    return t * cos + pltpu.roll(t, HEAD_DIM // 2, axis=1) * sin


def _retention_block(proj, lo_row, out_row, s, c, p, rstate_ref, mix_ref, *, chunk):
    cos = p.cos[c * chunk:(c + 1) * chunk, :]
    sin = p.sin[c * chunk:(c + 1) * chunk, :]
    heads = range(N_HEADS)
    col = lambda part, h: proj[lo_row:lo_row + chunk, part * RET_WIDTH + h * HEAD_DIM:
                               part * RET_WIDTH + (h + 1) * HEAD_DIM]
    qs = [_rope(col(0, h), cos, sin) for h in heads]
    ks = [_rope(col(1, h), cos, sin) for h in heads]
    vs = [col(2, h).astype(BF16) for h in heads]
    scs = [lax.dot_general(qs[h].astype(BF16), ks[h].astype(BF16), (((1,), (1,)), ((), ())),
                           preferred_element_type=F32) for h in heads]
    states = [rstate_ref[s, h] for h in heads]
    os = [jnp.dot((scs[h] * p.dmat[h]).astype(BF16), vs[h], preferred_element_type=F32)
          + jnp.dot((qs[h] * p.qdec[h]).astype(BF16), states[h].astype(BF16),
                    preferred_element_type=F32) for h in heads]
    for h in heads:
        kd = (ks[h] * p.kdec[h]).astype(BF16)
        rstate_ref[s, h] = (states[h] * p.cdec[h, 0:1, :]
                            + lax.dot_general(kd, vs[h], (((0,), (0,)), ((), ())),
                                              preferred_element_type=F32))
    for h in heads:
        lo = h * HEAD_DIM
        o, g = os[h], col(3, h)
        oc = o - jnp.mean(o, axis=-1, keepdims=True)
        on = oc * lax.rsqrt(jnp.mean(oc * oc, axis=-1, keepdims=True) + GN_EPS)
        ret = g * jax.nn.sigmoid(g) * (on * p.ret_w[:, lo:lo + HEAD_DIM])
        mix_ref[out_row:out_row + chunk, lo:lo + HEAD_DIM] = ret.astype(BF16)


def _conv_block(proj, lo_row, out_row, s, p, cstate_ref, mix_ref, ubuf_ref, *, chunk):
    base = 4 * RET_WIDTH
    gate_b = proj[lo_row:lo_row + chunk, base:base + CONV_WIDTH]
    gate_c = proj[lo_row:lo_row + chunk, base + CONV_WIDTH:base + 2 * CONV_WIDTH]
    hc = proj[lo_row:lo_row + chunk, base + 2 * CONV_WIDTH:base + 3 * CONV_WIDTH]
    u = gate_c * hc
    pad = F32_SUBLANES
    ubuf_ref[pad - (CONV_K - 1):pad, :] = cstate_ref[s]
    ubuf_ref[pad:pad + chunk, :] = u
    z = (ubuf_ref[pad - 2:pad - 2 + chunk, :] * p.conv_w[0:1, :]
         + ubuf_ref[pad - 1:pad - 1 + chunk, :] * p.conv_w[1:2, :]
         + u * p.conv_w[2:3, :])
    cstate_ref[s] = ubuf_ref[pad + chunk - (CONV_K - 1):pad + chunk, :]
    mix_ref[out_row:out_row + chunk, RET_WIDTH:] = (gate_b * z).astype(BF16)


def _mixer_stages(load_x, store_y, p, rstate_ref, cstate_ref, mix_ref, ubuf_ref,
                  *, nb, tm, chunk, n_groups):
    rows = nb * tm
    g_rows = rows // n_groups
    assert rows % n_groups == 0 and g_rows % chunk == 0 and tm % chunk == 0
    projs = []
    for g in range(n_groups):
        n = _rms_norm(load_x(g * g_rows, (g + 1) * g_rows), p.ln1[...]).astype(BF16)
        projs.append(jnp.dot(n, p.w_in[...], preferred_element_type=F32))
        yield
    for g in range(n_groups):
        r0 = g * g_rows
        for lo_row in range(0, g_rows, chunk):
            s, c = divmod(r0 + lo_row, tm)
            c //= chunk
            _retention_block(projs[g], lo_row, r0 + lo_row, s, c, p, rstate_ref, mix_ref, chunk=chunk)
            _conv_block(projs[g], lo_row, r0 + lo_row, s, p, cstate_ref, mix_ref, ubuf_ref, chunk=chunk)
        yield
        store_y(r0, r0 + g_rows, load_x(r0, r0 + g_rows)
                + jnp.dot(mix_ref[r0:r0 + g_rows, :], p.w_out[...], preferred_element_type=F32))
        if g + 1 < n_groups:
            yield


def _mlp_stages(load_x, store_y, p, *, rows, n_groups):
    g_rows = rows // n_groups
    assert rows % n_groups == 0
    acts = []
    for g in range(n_groups):
        h2 = _rms_norm(load_x(g * g_rows, (g + 1) * g_rows), p.ln2[...]).astype(BF16)
        up = jnp.dot(h2, p.w_up[...], preferred_element_type=F32)
        acts.append(jnp.square(jnp.maximum(up, 0.0)).astype(BF16))
        yield
    for g in range(n_groups):
        r0 = g * g_rows
        store_y(r0, r0 + g_rows, load_x(r0, r0 + g_rows)
                + jnp.dot(acts[g], p.w_down[...], preferred_element_type=F32))
        if g + 1 < n_groups:
            yield


def _final_norm_stages(load_y, store_y, lnf_ref, *, rows, n_groups):
    g_rows = rows // n_groups
    for g in range(n_groups):
        r0 = g * g_rows
        store_y(r0, r0 + g_rows, _rms_norm(load_y(r0, r0 + g_rows), lnf_ref[...]))
        if g + 1 < n_groups:
            yield


def _run_stages(order, **stages):
    for name in order:
        next(stages[name], None)
    for gen in stages.values():
        assert next(gen, "done") == "done"


def _run_all(stages):
    for _ in stages:
        pass


MLP_ROW_GROUPS = 1
FUSED_STAGE_ORDER = "MLM" "MLM" "MM"
FUSED_STAGE_ORDER_FINAL_NORM = "MLM" "NN" "MLM" "MM"
MLP_NORM_STAGE_ORDER = "LNNL"


def _prompt_layer_kernel(x_ref, ln1_ref, win_ref, cos_ref, sin_ref, dmat_ref, qdec_ref, kdec_ref,
                         cdec_ref, convw_ref, retw_ref, wout_ref, ln2_ref, wup_ref, wdown_ref, lnf_ref,
                         *rest, n_tiles, tiles_per_seq, tm, chunk, final_norm, state_layer):
    if final_norm:
        xo_ref, rout_ref, cout_ref, mix_ref, ubuf_ref, x1_ref, yraw_ref = rest[-7:]
    else:
        xo_ref, rout_ref, cout_ref, mix_ref, ubuf_ref, x1_ref = rest[-6:]
    t = pl.program_id(0)
    mixer_refs = MixerRefs(ln1_ref, win_ref, cos_ref, sin_ref, dmat_ref, qdec_ref, kdec_ref, cdec_ref,
                           convw_ref, retw_ref, wout_ref)
    mlp_refs = MlpRefs(ln2_ref, wup_ref, wdown_ref, lnf_ref)

    @pl.when(jnp.logical_and(lax.rem(t, tiles_per_seq) == 0, t < n_tiles))
    def _():
        rout_ref[...] = jnp.zeros(rout_ref.shape, F32)
        cout_ref[...] = jnp.zeros(cout_ref.shape, F32)

    rnew_ref, cnew_ref = _own_layer_state(rout_ref, cout_ref, state_layer)

    cur = lax.rem(t, 2)
    prev = 1 - cur

    def store_x1(r0, r1, y):
        x1_ref[cur, r0:r1, :] = y

    def store_out(r0, r1, y):
        xo_ref[r0:r1, :] = y

    def store_raw(r0, r1, y):
        yraw_ref[r0:r1, :] = y

    n_groups = tm // chunk

    def mixer():
        return _mixer_stages(lambda r0, r1: x_ref[r0:r1, :], store_x1, mixer_refs, rnew_ref, cnew_ref,
                             mix_ref, ubuf_ref, nb=1, tm=tm, chunk=chunk, n_groups=n_groups)

    def mlp():
        return _mlp_stages(lambda r0, r1: x1_ref[prev, r0:r1, :], store_raw if final_norm else store_out,
                           mlp_refs, rows=tm, n_groups=MLP_ROW_GROUPS)

    def norm():
        return _final_norm_stages(lambda r0, r1: yraw_ref[r0:r1, :], store_out, lnf_ref,
                                  rows=tm, n_groups=n_groups)

    if not final_norm:
        @pl.when(t == 0)
        def _():
            _run_all(mixer())

        @pl.when(jnp.logical_and(t > 0, t < n_tiles))
        def _():
            _run_stages(FUSED_STAGE_ORDER, M=mixer(), L=mlp())

        @pl.when(t == n_tiles)
        def _():
            _run_all(mlp())
    else:
        @pl.when(t == 0)
        def _():
            yraw_ref[...] = jnp.zeros(yraw_ref.shape, F32)
            _run_all(mixer())

        @pl.when(jnp.logical_and(t > 0, t < n_tiles))
        def _():
            _run_stages(FUSED_STAGE_ORDER_FINAL_NORM, M=mixer(), L=mlp(), N=norm())

        @pl.when(t == n_tiles)
        def _():
            _run_stages(MLP_NORM_STAGE_ORDER, L=mlp(), N=norm())

        @pl.when(t == n_tiles + 1)
        def _():
            _run_all(norm())


def _stacked_state_plumbing(prev_states, n_fixed_inputs, layer, depth, state_block, state_index):
    if not prev_states:
        assert layer == 0
        out_specs = [pl.BlockSpec((depth,) + state_block(k), lambda i, k=k: (0,) + state_index(k, i))
                     for k in ("ret", "conv")]
        return [], out_specs, {}, layer
    out_specs = [pl.BlockSpec((None,) + state_block(k), lambda i, k=k: (layer,) + state_index(k, i))
                 for k in ("ret", "conv")]
    extra_specs = [pl.BlockSpec(memory_space=pl.ANY) for _ in prev_states]
    aliases = {n_fixed_inputs + j: 1 + j for j in range(len(prev_states))}
    return extra_specs, out_specs, aliases, None


def _own_layer_state(rout_ref, cout_ref, state_layer):
    if state_layer is None:
        return rout_ref, cout_ref
    return rout_ref.at[state_layer], cout_ref.at[state_layer]


def _prompt_layer_call(x2d, seq_len, layer, depth, params, prev_states, *, final_norm):
    rows, _ = x2d.shape
    tm, chunk = PROMPT_TILE, PROMPT_CHUNK
    assert tm == 2 * chunk, "the stage orders are written for two row groups per tile"
    assert seq_len % tm == 0 and rows % seq_len == 0 and chunk % 16 == 0
    tiles_per_seq = seq_len // tm
    n_tiles = rows // tm
    n_seq = rows // seq_len
    cos2, sin2 = _rope_tables(0, seq_len)
    dmat, qdec, kdec, cdec = _decay_tables(chunk)

    def tile_in(t):
        return jnp.minimum(t, n_tiles - 1)

    out_lag = 2 if final_norm else 1

    def tile_out(t):
        return jnp.clip(t - out_lag, 0, n_tiles - 1)

    const = lambda shape: pl.BlockSpec(shape, lambda t: (0,) * len(shape))
    per_layer = lambda shape: pl.BlockSpec((None,) + shape, lambda t: (layer,) + (0,) * len(shape))
    state_blocks = {"ret": (1, N_HEADS, HEAD_DIM, HEAD_DIM), "conv": (1, CONV_K - 1, CONV_WIDTH)}
    extra_specs, state_specs, aliases, state_layer = _stacked_state_plumbing(
        prev_states, 16, layer, depth, state_blocks.get,
        lambda kind, t: (tile_in(t) // tiles_per_seq,) + (0,) * (len(state_blocks[kind]) - 1))
    kernel = functools.partial(_prompt_layer_kernel, n_tiles=n_tiles, tiles_per_seq=tiles_per_seq,
                               tm=tm, chunk=chunk, final_norm=final_norm, state_layer=state_layer)
    return pl.pallas_call(
        kernel,
        grid=(n_tiles + out_lag,),
        in_specs=[
            pl.BlockSpec((tm, D_MODEL), lambda t: (tile_in(t), 0)),
            per_layer((1, D_MODEL)),
            per_layer((D_MODEL, IN_COLS)),
            pl.BlockSpec((tm, HEAD_DIM), lambda t: (tile_in(t) % tiles_per_seq, 0)),
            pl.BlockSpec((tm, HEAD_DIM), lambda t: (tile_in(t) % tiles_per_seq, 0)),
            const((N_HEADS, chunk, chunk)),
            const((N_HEADS, chunk, HEAD_DIM)),
            const((N_HEADS, chunk, HEAD_DIM)),
            const((N_HEADS, F32_SUBLANES, HEAD_DIM)),
            per_layer((CONV_K, CONV_WIDTH)),
            per_layer((1, RET_WIDTH)),
            per_layer((D_MODEL, D_MODEL)),
            per_layer((1, D_MODEL)),
            per_layer((D_MODEL, D_FF)),
            per_layer((D_FF, D_MODEL)),
            const((1, D_MODEL)),
        ] + extra_specs,
        out_specs=[pl.BlockSpec((tm, D_MODEL), lambda t: (tile_out(t), 0))] + state_specs,
        out_shape=[
            jax.ShapeDtypeStruct((rows, D_MODEL), F32),
            jax.ShapeDtypeStruct((depth, n_seq, N_HEADS, HEAD_DIM, HEAD_DIM), F32),
            jax.ShapeDtypeStruct((depth, n_seq, CONV_K - 1, CONV_WIDTH), F32),
        ],
        input_output_aliases=aliases,
        scratch_shapes=[
            pltpu.VMEM((tm, D_MODEL), BF16),
            pltpu.VMEM((F32_SUBLANES + chunk, CONV_WIDTH), F32),
            pltpu.VMEM((2, tm, D_MODEL), F32),
        ] + ([pltpu.VMEM((tm, D_MODEL), F32)] if final_norm else []),
        compiler_params=pltpu.CompilerParams(
            dimension_semantics=("arbitrary",),
            vmem_limit_bytes=V7X_VMEM_LIMIT_BYTES),
        name="prompt_layer",
    )(x2d, params["ln1"], params["w_in"], cos2, sin2, dmat, qdec, kdec, cdec, params["conv_w"],
      params["ret_w"], params["w_out"], params["ln2"], params["w_up"], params["w_down"], params["ln_f"],
      *prev_states)


def _sample_mixer_kernel(x_ref, ln1_ref, win_ref, cos_ref, sin_ref, dmat_ref, qdec_ref, kdec_ref,
                         cdec_ref, convw_ref, retw_ref, wout_ref, r0_ref, cb0_ref, *rest, nb, tm,
                         state_layer):
    xo_ref, rout_ref, cout_ref, mix_ref, ubuf_ref = rest[-5:]
    mixer_refs = MixerRefs(ln1_ref, win_ref, cos_ref, sin_ref, dmat_ref, qdec_ref, kdec_ref, cdec_ref,
                           convw_ref, retw_ref, wout_ref)
    if state_layer is not None:
        rout_ref[...] = jnp.zeros(rout_ref.shape, F32)
        cout_ref[...] = jnp.zeros(cout_ref.shape, F32)
    rnew_ref, cnew_ref = _own_layer_state(rout_ref, cout_ref, state_layer)
    rnew_ref[...] = r0_ref[...]
    cnew_ref[...] = cb0_ref[...]

    def store_out(r0, r1, y):
        xo_ref[r0:r1, :] = y

    _run_all(_mixer_stages(lambda r0, r1: x_ref[r0:r1, :], store_out, mixer_refs, rnew_ref, cnew_ref,
                           mix_ref, ubuf_ref, nb=nb, tm=tm, chunk=tm, n_groups=1))


def _sample_mixer_call(x2d, r_all, c_all, layer, depth, params, prev_states, *, nb, tm):
    assert tm % 16 == 0 and x2d.shape[0] == nb * tm
    state_blocks = {"ret": (nb, N_HEADS, HEAD_DIM, HEAD_DIM), "conv": (nb, CONV_K - 1, CONV_WIDTH)}
    extra_specs, state_specs, aliases, state_layer = _stacked_state_plumbing(
        prev_states, 14, layer, depth, state_blocks.get, lambda kind, i: (0,) * len(state_blocks[kind]))
    cos2, sin2 = _rope_tables(PAST_LEN, tm)
    dmat, qdec, kdec, cdec = _decay_tables(tm)
    const = lambda shape: pl.BlockSpec(shape, lambda i: (0,) * len(shape))
    per_layer = lambda shape: pl.BlockSpec((None,) + shape, lambda i: (layer,) + (0,) * len(shape))
    return pl.pallas_call(
        functools.partial(_sample_mixer_kernel, nb=nb, tm=tm, state_layer=state_layer),
        grid=(1,),
        in_specs=[
            const((nb * tm, D_MODEL)),
            per_layer((1, D_MODEL)),
            per_layer((D_MODEL, IN_COLS)),
            const((tm, HEAD_DIM)),
            const((tm, HEAD_DIM)),
            const((N_HEADS, tm, tm)),
            const((N_HEADS, tm, HEAD_DIM)),
            const((N_HEADS, tm, HEAD_DIM)),
            const((N_HEADS, F32_SUBLANES, HEAD_DIM)),
            per_layer((CONV_K, CONV_WIDTH)),
            per_layer((1, RET_WIDTH)),
            per_layer((D_MODEL, D_MODEL)),
            per_layer((nb, N_HEADS, HEAD_DIM, HEAD_DIM)),
            per_layer((nb, CONV_K - 1, CONV_WIDTH)),
        ] + extra_specs,
        out_specs=[const((nb * tm, D_MODEL))] + state_specs,
        out_shape=[
            jax.ShapeDtypeStruct((nb * tm, D_MODEL), F32),
            jax.ShapeDtypeStruct((depth, nb, N_HEADS, HEAD_DIM, HEAD_DIM), F32),
            jax.ShapeDtypeStruct((depth, nb, CONV_K - 1, CONV_WIDTH), F32),
        ],
        input_output_aliases=aliases,
        scratch_shapes=[
            pltpu.VMEM((nb * tm, D_MODEL), BF16),
            pltpu.VMEM((F32_SUBLANES + tm, CONV_WIDTH), F32),
        ],
        compiler_params=pltpu.CompilerParams(
            dimension_semantics=("arbitrary",),
            vmem_limit_bytes=V7X_VMEM_LIMIT_BYTES),
        name="sample_mixer",
    )(x2d, params["ln1"], params["w_in"], cos2, sin2, dmat, qdec, kdec, cdec, params["conv_w"],
      params["ret_w"], params["w_out"], r_all, c_all, *prev_states)


def _sample_mlp_kernel(x_ref, ln2_ref, wup_ref, wdown_ref, lnf_ref, o_ref, h2_ref, *, n_chunks, final_norm):
    c = pl.program_id(0)

    @pl.when(c == 0)
    def _():
        x = x_ref[...]
        h2_ref[...] = _rms_norm(x, ln2_ref[...]).astype(BF16)
        o_ref[...] = x

    up = jnp.dot(h2_ref[...], wup_ref[...], preferred_element_type=F32)
    act = jnp.square(jnp.maximum(up, 0.0)).astype(BF16)
    o_ref[...] += jnp.dot(act, wdown_ref[...], preferred_element_type=F32)

    if final_norm:
        @pl.when(c == n_chunks - 1)
        def _():
            o_ref[...] = _rms_norm(o_ref[...], lnf_ref[...])


def _sample_mlp_call(x2d, layer, params, *, final_norm):
    t, _ = x2d.shape
    n_chunks = D_FF // SAMPLE_FF_CHUNK
    const = lambda shape: pl.BlockSpec(shape, lambda c: (0,) * len(shape))
    return pl.pallas_call(
        functools.partial(_sample_mlp_kernel, n_chunks=n_chunks, final_norm=final_norm),
        grid=(n_chunks,),
        in_specs=[
            const((t, D_MODEL)),
            pl.BlockSpec((None, 1, D_MODEL), lambda c: (layer, 0, 0)),
            pl.BlockSpec((None, D_MODEL, SAMPLE_FF_CHUNK), lambda c: (layer, 0, c)),
            pl.BlockSpec((None, SAMPLE_FF_CHUNK, D_MODEL), lambda c: (layer, c, 0)),
            const((1, D_MODEL)),
        ],
        out_specs=const((t, D_MODEL)),
        out_shape=jax.ShapeDtypeStruct((t, D_MODEL), F32),
        scratch_shapes=[pltpu.VMEM((t, D_MODEL), BF16)],
        compiler_params=pltpu.CompilerParams(
            dimension_semantics=("arbitrary",),
            vmem_limit_bytes=V7X_VMEM_LIMIT_BYTES),
        name="sample_mlp",
    )(x2d, params["ln2"], params["w_up"], params["w_down"], params["ln_f"])


def kernel(x_prompt, x_sample, state_ret, state_conv, ln1_w, w_in, conv_w, ret_norm_w, w_out, ln2_w, w_mlp_up, w_mlp_down, ln_f_w):
    depth = w_in.shape[0]
    b_p, l_p, _ = x_prompt.shape
    b_s, l_s, _ = x_sample.shape

    def to_bf16(w):
        return w.reshape(-1, w.shape[-1]).astype(BF16).reshape(w.shape)

    params = dict(
        ln1=ln1_w.reshape(depth, 1, D_MODEL),
        w_in=to_bf16(w_in),
        conv_w=jnp.swapaxes(conv_w, 1, 2),
        ret_w=ret_norm_w.reshape(depth, 1, RET_WIDTH),
        w_out=to_bf16(w_out),
        ln2=ln2_w.reshape(depth, 1, D_MODEL),
        w_up=to_bf16(w_mlp_up),
        w_down=to_bf16(w_mlp_down),
        ln_f=ln_f_w.reshape(1, D_MODEL),
    )

    xp = x_prompt.reshape(b_p * l_p, D_MODEL)
    xs = x_sample.reshape(b_s * l_s, D_MODEL)
    states_p, states_s = (), ()
    for i in range(depth):
        last = i == depth - 1
        xp, *states_p = _prompt_layer_call(xp, l_p, i, depth, params, states_p, final_norm=last)
        xs, *states_s = _sample_mixer_call(xs, state_ret, state_conv, i, depth, params, states_s,
                                           nb=b_s, tm=l_s)
        xs = _sample_mlp_call(xs, i, params, final_norm=last)

    return (xp.reshape(b_p, l_p, D_MODEL), xs.reshape(b_s, l_s, D_MODEL),
            states_p[0], states_p[1], states_s[0], states_s[1])
```

```python
import functools
from typing import Any, NamedTuple

import numpy as np
import jax
import jax.numpy as jnp
from jax import lax
from jax.experimental import pallas as pl
from jax.experimental.pallas import tpu as pltpu

D_MODEL = 1024
N_HEADS = 4
HEAD_DIM = 128
RET_WIDTH = N_HEADS * HEAD_DIM
CONV_WIDTH = D_MODEL - RET_WIDTH
CONV_K = 3
D_FF = 4 * D_MODEL
IN_COLS = 4 * RET_WIDTH + 3 * CONV_WIDTH
PAST_LEN = 2048
ROPE_BASE = 10000.0
RMS_EPS = 1e-6
GN_EPS = 1e-5

V7X_VMEM_LIMIT_BYTES = 60 * 1024 * 1024
F32_SUBLANES = 8
PROMPT_TILE = 512
PROMPT_CHUNK = 256
SAMPLE_FF_CHUNK = 1024

BF16 = jnp.bfloat16
F32 = jnp.float32


def _rope_tables(pos0, length):
    half = HEAD_DIM // 2
    inv = ROPE_BASE ** (-np.arange(half, dtype=np.float64) / half)
    ang = (pos0 + np.arange(length, dtype=np.float64))[:, None] * inv[None, :]
    cos, sin = np.cos(ang), np.sin(ang)
    cos2 = np.concatenate([cos, cos], axis=-1)
    sin2 = np.concatenate([-sin, sin], axis=-1)
    return jnp.asarray(cos2, F32), jnp.asarray(sin2, F32)


def _decay_tables(chunk):
    lg = np.log(1.0 - np.exp2(-5.0 - np.arange(N_HEADS, dtype=np.float64)))
    idx = np.arange(chunk, dtype=np.float64)
    diff = idx[:, None] - idx[None, :]
    scale = HEAD_DIM ** -0.5
    dmat = np.where(diff >= 0, np.exp(np.maximum(diff, 0.0)[None] * lg[:, None, None]), 0.0) * scale
    qdec = np.exp((idx[None, :] + 1.0) * lg[:, None])
    kdec = np.exp((chunk - 1.0 - idx[None, :]) * lg[:, None]) * scale
    cdec = np.exp(chunk * lg)
    lanes = np.ones((1, 1, HEAD_DIM))
    return (jnp.asarray(dmat, F32),
            jnp.asarray(qdec[:, :, None] * lanes, F32),
            jnp.asarray(kdec[:, :, None] * lanes, F32),
            jnp.asarray(cdec[:, None, None] * np.ones((1, F32_SUBLANES, HEAD_DIM)), F32))


def _rms_norm(x, w):
    return x * lax.rsqrt(jnp.mean(x * x, axis=-1, keepdims=True) + RMS_EPS) * w


class MixerRefs(NamedTuple):
    ln1: Any
    w_in: Any
    cos: Any
    sin: Any
    dmat: Any
    qdec: Any
    kdec: Any
    cdec: Any
    conv_w: Any
    ret_w: Any
    w_out: Any


class MlpRefs(NamedTuple):
    ln2: Any
    w_up: Any
    w_down: Any
    ln_f: Any


def _rope(t, cos, sin):
    return t * cos + pltpu.roll(t, HEAD_DIM // 2, axis=1) * sin


def _retention_block(proj, lo_row, out_row, s, c, p, rstate_ref, mix_ref, *, chunk):
    cos = p.cos[c * chunk:(c + 1) * chunk, :]
    sin = p.sin[c * chunk:(c + 1) * chunk, :]
    heads = range(N_HEADS)
    col = lambda part, h: proj[lo_row:lo_row + chunk, part * RET_WIDTH + h * HEAD_DIM:
                               part * RET_WIDTH + (h + 1) * HEAD_DIM]
    qs = [_rope(col(0, h), cos, sin) for h in heads]
    ks = [_rope(col(1, h), cos, sin) for h in heads]
    vs = [col(2, h).astype(BF16) for h in heads]
    scs = [lax.dot_general(qs[h].astype(BF16), ks[h].astype(BF16), (((1,), (1,)), ((), ())),
                           preferred_element_type=F32) for h in heads]
    states = [rstate_ref[s, h] for h in heads]
    os = [jnp.dot((scs[h] * p.dmat[h]).astype(BF16), vs[h], preferred_element_type=F32)
          + jnp.dot((qs[h] * p.qdec[h]).astype(BF16), states[h].astype(BF16),
                    preferred_element_type=F32) for h in heads]
    for h in heads:
        kd = (ks[h] * p.kdec[h]).astype(BF16)
        rstate_ref[s, h] = (states[h] * p.cdec[h, 0:1, :]
                            + lax.dot_general(kd, vs[h], (((0,), (0,)), ((), ())),
                                              preferred_element_type=F32))
    for h in heads:
        lo = h * HEAD_DIM
        o, g = os[h], col(3, h)
        oc = o - jnp.mean(o, axis=-1, keepdims=True)
        on = oc * lax.rsqrt(jnp.mean(oc * oc, axis=-1, keepdims=True) + GN_EPS)
        ret = g * jax.nn.sigmoid(g) * (on * p.ret_w[:, lo:lo + HEAD_DIM])
        mix_ref[out_row:out_row + chunk, lo:lo + HEAD_DIM] = ret.astype(BF16)


def _conv_block(proj, lo_row, out_row, s, p, cstate_ref, mix_ref, ubuf_ref, *, chunk):
    base = 4 * RET_WIDTH
    gate_b = proj[lo_row:lo_row + chunk, base:base + CONV_WIDTH]
    gate_c = proj[lo_row:lo_row + chunk, base + CONV_WIDTH:base + 2 * CONV_WIDTH]
    hc = proj[lo_row:lo_row + chunk, base + 2 * CONV_WIDTH:base + 3 * CONV_WIDTH]
    u = gate_c * hc
    pad = F32_SUBLANES
    ubuf_ref[pad - (CONV_K - 1):pad, :] = cstate_ref[s]
    ubuf_ref[pad:pad + chunk, :] = u
    z = (ubuf_ref[pad - 2:pad - 2 + chunk, :] * p.conv_w[0:1, :]
         + ubuf_ref[pad - 1:pad - 1 + chunk, :] * p.conv_w[1:2, :]
         + u * p.conv_w[2:3, :])
    cstate_ref[s] = ubuf_ref[pad + chunk - (CONV_K - 1):pad + chunk, :]
    mix_ref[out_row:out_row + chunk, RET_WIDTH:] = (gate_b * z).astype(BF16)


def _mixer_stages(load_x, store_y, p, rstate_ref, cstate_ref, mix_ref, ubuf_ref,
                  *, nb, tm, chunk, n_groups):
    rows = nb * tm
    g_rows = rows // n_groups
    assert rows % n_groups == 0 and g_rows % chunk == 0 and tm % chunk == 0
    projs = []
    for g in range(n_groups):
        n = _rms_norm(load_x(g * g_rows, (g + 1) * g_rows), p.ln1[...]).astype(BF16)
        projs.append(jnp.dot(n, p.w_in[...], preferred_element_type=F32))
        yield
    for g in range(n_groups):
        r0 = g * g_rows
        for lo_row in range(0, g_rows, chunk):
            s, c = divmod(r0 + lo_row, tm)
            c //= chunk
            _retention_block(projs[g], lo_row, r0 + lo_row, s, c, p, rstate_ref, mix_ref, chunk=chunk)
            _conv_block(projs[g], lo_row, r0 + lo_row, s, p, cstate_ref, mix_ref, ubuf_ref, chunk=chunk)
        yield
        store_y(r0, r0 + g_rows, load_x(r0, r0 + g_rows)
                + jnp.dot(mix_ref[r0:r0 + g_rows, :], p.w_out[...], preferred_element_type=F32))
        if g + 1 < n_groups:
            yield


def _mlp_input(x, ln2_ref):
    return _rms_norm(x, ln2_ref[...]).astype(BF16)


def _mlp_stages(load_x, load_h2, store_y, p, *, rows, n_groups):
    g_rows = rows // n_groups
    assert rows % n_groups == 0
    acts = []
    for g in range(n_groups):
        up = jnp.dot(load_h2(g * g_rows, (g + 1) * g_rows), p.w_up[...], preferred_element_type=F32)
        acts.append(jnp.square(jnp.maximum(up, 0.0)).astype(BF16))
        yield
    for g in range(n_groups):
        r0 = g * g_rows
        store_y(r0, r0 + g_rows, load_x(r0, r0 + g_rows)
                + jnp.dot(acts[g], p.w_down[...], preferred_element_type=F32))
        if g + 1 < n_groups:
            yield


def _final_norm_stages(load_y, store_y, lnf_ref, *, rows, n_groups):
    g_rows = rows // n_groups
    for g in range(n_groups):
        r0 = g * g_rows
        store_y(r0, r0 + g_rows, _rms_norm(load_y(r0, r0 + g_rows), lnf_ref[...]))
        if g + 1 < n_groups:
            yield


def _run_stages(order, **stages):
    for name in order:
        next(stages[name], None)
    for gen in stages.values():
        assert next(gen, "done") == "done"


def _run_all(stages):
    for _ in stages:
        pass


FUSED_STAGE_ORDER = "LMLMMLMMLM"
FUSED_STAGE_ORDER_FINAL_NORM = "LMLMNMLMNMLM"
MLP_NORM_STAGE_ORDER = "LLNLNL"


def _prompt_layer_kernel(x_ref, ln1_ref, win_ref, cos_ref, sin_ref, dmat_ref, qdec_ref, kdec_ref,
                         cdec_ref, convw_ref, retw_ref, wout_ref, ln2_ref, wup_ref, wdown_ref, lnf_ref,
                         *rest, n_tiles, tiles_per_seq, tm, chunk, final_norm, state_layer):
    if final_norm:
        xo_ref, rout_ref, cout_ref, mix_ref, ubuf_ref, x1_ref, h2a_ref, yraw_ref = rest[-8:]
    else:
        xo_ref, rout_ref, cout_ref, mix_ref, ubuf_ref, x1_ref, h2a_ref = rest[-7:]
    t = pl.program_id(0)
    mixer_refs = MixerRefs(ln1_ref, win_ref, cos_ref, sin_ref, dmat_ref, qdec_ref, kdec_ref, cdec_ref,
                           convw_ref, retw_ref, wout_ref)
    mlp_refs = MlpRefs(ln2_ref, wup_ref, wdown_ref, lnf_ref)

    @pl.when(jnp.logical_and(lax.rem(t, tiles_per_seq) == 0, t < n_tiles))
    def _():
        rout_ref[...] = jnp.zeros(rout_ref.shape, F32)
        cout_ref[...] = jnp.zeros(cout_ref.shape, F32)

    rnew_ref, cnew_ref = _own_layer_state(rout_ref, cout_ref, state_layer)

    cur = lax.rem(t, 2)
    prev = 1 - cur

    def store_x1(r0, r1, y):
        x1_ref[cur, r0:r1, :] = y
        if r0 == 0:
            h2a_ref[...] = _mlp_input(y, ln2_ref)

    def store_out(r0, r1, y):
        xo_ref[r0:r1, :] = y

    def store_raw(r0, r1, y):
        yraw_ref[r0:r1, :] = y

    n_groups = tm // chunk

    def mixer():
        return _mixer_stages(lambda r0, r1: x_ref[r0:r1, :], store_x1, mixer_refs, rnew_ref, cnew_ref,
                             mix_ref, ubuf_ref, nb=1, tm=tm, chunk=chunk, n_groups=n_groups)

    def load_h2(r0, r1):
        assert r1 - r0 == chunk
        return h2a_ref[...] if r0 == 0 else _mlp_input(x1_ref[prev, r0:r1, :], ln2_ref)

    def mlp():
        return _mlp_stages(lambda r0, r1: x1_ref[prev, r0:r1, :], load_h2,
                           store_raw if final_norm else store_out, mlp_refs, rows=tm, n_groups=n_groups)

    def norm():
        return _final_norm_stages(lambda r0, r1: yraw_ref[r0:r1, :], store_out, lnf_ref,
                                  rows=tm, n_groups=n_groups)

    if not final_norm:
        @pl.when(t == 0)
        def _():
            _run_all(mixer())

        @pl.when(jnp.logical_and(t > 0, t < n_tiles))
        def _():
            _run_stages(FUSED_STAGE_ORDER, M=mixer(), L=mlp())

        @pl.when(t == n_tiles)
        def _():
            _run_all(mlp())
    else:
        @pl.when(t == 0)
        def _():
            yraw_ref[...] = jnp.zeros(yraw_ref.shape, F32)
            _run_all(mixer())

        @pl.when(jnp.logical_and(t > 0, t < n_tiles))
        def _():
            _run_stages(FUSED_STAGE_ORDER_FINAL_NORM, M=mixer(), L=mlp(), N=norm())

        @pl.when(t == n_tiles)
        def _():
            _run_stages(MLP_NORM_STAGE_ORDER, L=mlp(), N=norm())

        @pl.when(t == n_tiles + 1)
        def _():
            _run_all(norm())


def _stacked_state_plumbing(prev_states, n_fixed_inputs, layer, depth, state_block, state_index):
    if not prev_states:
        assert layer == 0
        out_specs = [pl.BlockSpec((depth,) + state_block(k), lambda i, k=k: (0,) + state_index(k, i))
                     for k in ("ret", "conv")]
        return [], out_specs, {}, layer
    out_specs = [pl.BlockSpec((None,) + state_block(k), lambda i, k=k: (layer,) + state_index(k, i))
                 for k in ("ret", "conv")]
    extra_specs = [pl.BlockSpec(memory_space=pl.ANY) for _ in prev_states]
    aliases = {n_fixed_inputs + j: 1 + j for j in range(len(prev_states))}
    return extra_specs, out_specs, aliases, None


def _own_layer_state(rout_ref, cout_ref, state_layer):
    if state_layer is None:
        return rout_ref, cout_ref
    return rout_ref.at[state_layer], cout_ref.at[state_layer]


def _prompt_layer_call(x2d, seq_len, layer, depth, params, prev_states, *, final_norm):
    rows, _ = x2d.shape
    tm, chunk = PROMPT_TILE, PROMPT_CHUNK
    assert tm == 2 * chunk, "the stage orders are written for two row groups per tile"
    assert seq_len % tm == 0 and rows % seq_len == 0 and chunk % 16 == 0
    tiles_per_seq = seq_len // tm
    n_tiles = rows // tm
    n_seq = rows // seq_len
    cos2, sin2 = _rope_tables(0, seq_len)
    dmat, qdec, kdec, cdec = _decay_tables(chunk)

    def tile_in(t):
        return jnp.minimum(t, n_tiles - 1)

    out_lag = 2 if final_norm else 1

    def tile_out(t):
        return jnp.clip(t - out_lag, 0, n_tiles - 1)

    const = lambda shape: pl.BlockSpec(shape, lambda t: (0,) * len(shape))
    per_layer = lambda shape: pl.BlockSpec((None,) + shape, lambda t: (layer,) + (0,) * len(shape))
    state_blocks = {"ret": (1, N_HEADS, HEAD_DIM, HEAD_DIM), "conv": (1, CONV_K - 1, CONV_WIDTH)}
    extra_specs, state_specs, aliases, state_layer = _stacked_state_plumbing(
        prev_states, 16, layer, depth, state_blocks.get,
        lambda kind, t: (tile_in(t) // tiles_per_seq,) + (0,) * (len(state_blocks[kind]) - 1))
    kernel = functools.partial(_prompt_layer_kernel, n_tiles=n_tiles, tiles_per_seq=tiles_per_seq,
                               tm=tm, chunk=chunk, final_norm=final_norm, state_layer=state_layer)
    return pl.pallas_call(
        kernel,
        grid=(n_tiles + out_lag,),
        in_specs=[
            pl.BlockSpec((tm, D_MODEL), lambda t: (tile_in(t), 0)),
            per_layer((1, D_MODEL)),
            per_layer((D_MODEL, IN_COLS)),
            pl.BlockSpec((tm, HEAD_DIM), lambda t: (tile_in(t) % tiles_per_seq, 0)),
            pl.BlockSpec((tm, HEAD_DIM), lambda t: (tile_in(t) % tiles_per_seq, 0)),
            const((N_HEADS, chunk, chunk)),
            const((N_HEADS, chunk, HEAD_DIM)),
            const((N_HEADS, chunk, HEAD_DIM)),
            const((N_HEADS, F32_SUBLANES, HEAD_DIM)),
            per_layer((CONV_K, CONV_WIDTH)),
            per_layer((1, RET_WIDTH)),
            per_layer((D_MODEL, D_MODEL)),
            per_layer((1, D_MODEL)),
            per_layer((D_MODEL, D_FF)),
            per_layer((D_FF, D_MODEL)),
            const((1, D_MODEL)),
        ] + extra_specs,
        out_specs=[pl.BlockSpec((tm, D_MODEL), lambda t: (tile_out(t), 0))] + state_specs,
        out_shape=[
            jax.ShapeDtypeStruct((rows, D_MODEL), F32),
            jax.ShapeDtypeStruct((depth, n_seq, N_HEADS, HEAD_DIM, HEAD_DIM), F32),
            jax.ShapeDtypeStruct((depth, n_seq, CONV_K - 1, CONV_WIDTH), F32),
        ],
        input_output_aliases=aliases,
        scratch_shapes=[
            pltpu.VMEM((tm, D_MODEL), BF16),
            pltpu.VMEM((F32_SUBLANES + chunk, CONV_WIDTH), F32),
            pltpu.VMEM((2, tm, D_MODEL), F32),
            pltpu.VMEM((chunk, D_MODEL), BF16),
        ] + ([pltpu.VMEM((tm, D_MODEL), F32)] if final_norm else []),
        compiler_params=pltpu.CompilerParams(
            dimension_semantics=("arbitrary",),
            vmem_limit_bytes=V7X_VMEM_LIMIT_BYTES),
        name="prompt_layer",
    )(x2d, params["ln1"], params["w_in"], cos2, sin2, dmat, qdec, kdec, cdec, params["conv_w"],
      params["ret_w"], params["w_out"], params["ln2"], params["w_up"], params["w_down"], params["ln_f"],
      *prev_states)


def _sample_mixer_kernel(x_ref, ln1_ref, win_ref, cos_ref, sin_ref, dmat_ref, qdec_ref, kdec_ref,
                         cdec_ref, convw_ref, retw_ref, wout_ref, r0_ref, cb0_ref, *rest, nb, tm,
                         state_layer):
    xo_ref, rout_ref, cout_ref, mix_ref, ubuf_ref = rest[-5:]
    mixer_refs = MixerRefs(ln1_ref, win_ref, cos_ref, sin_ref, dmat_ref, qdec_ref, kdec_ref, cdec_ref,
                           convw_ref, retw_ref, wout_ref)
    if state_layer is not None:
        rout_ref[...] = jnp.zeros(rout_ref.shape, F32)
        cout_ref[...] = jnp.zeros(cout_ref.shape, F32)
    rnew_ref, cnew_ref = _own_layer_state(rout_ref, cout_ref, state_layer)
    rnew_ref[...] = r0_ref[...]
    cnew_ref[...] = cb0_ref[...]

    def store_out(r0, r1, y):
        xo_ref[r0:r1, :] = y

    _run_all(_mixer_stages(lambda r0, r1: x_ref[r0:r1, :], store_out, mixer_refs, rnew_ref, cnew_ref,
                           mix_ref, ubuf_ref, nb=nb, tm=tm, chunk=tm, n_groups=1))


def _sample_mixer_call(x2d, r_all, c_all, layer, depth, params, prev_states, *, nb, tm):
    assert tm % 16 == 0 and x2d.shape[0] == nb * tm
    state_blocks = {"ret": (nb, N_HEADS, HEAD_DIM, HEAD_DIM), "conv": (nb, CONV_K - 1, CONV_WIDTH)}
    extra_specs, state_specs, aliases, state_layer = _stacked_state_plumbing(
        prev_states, 14, layer, depth, state_blocks.get, lambda kind, i: (0,) * len(state_blocks[kind]))
    cos2, sin2 = _rope_tables(PAST_LEN, tm)
    dmat, qdec, kdec, cdec = _decay_tables(tm)
    const = lambda shape: pl.BlockSpec(shape, lambda i: (0,) * len(shape))
    per_layer = lambda shape: pl.BlockSpec((None,) + shape, lambda i: (layer,) + (0,) * len(shape))
    return pl.pallas_call(
        functools.partial(_sample_mixer_kernel, nb=nb, tm=tm, state_layer=state_layer),
        grid=(1,),
        in_specs=[
            const((nb * tm, D_MODEL)),
            per_layer((1, D_MODEL)),
            per_layer((D_MODEL, IN_COLS)),
            const((tm, HEAD_DIM)),
            const((tm, HEAD_DIM)),
            const((N_HEADS, tm, tm)),
            const((N_HEADS, tm, HEAD_DIM)),
            const((N_HEADS, tm, HEAD_DIM)),
            const((N_HEADS, F32_SUBLANES, HEAD_DIM)),
            per_layer((CONV_K, CONV_WIDTH)),
            per_layer((1, RET_WIDTH)),
            per_layer((D_MODEL, D_MODEL)),
            per_layer((nb, N_HEADS, HEAD_DIM, HEAD_DIM)),
            per_layer((nb, CONV_K - 1, CONV_WIDTH)),
        ] + extra_specs,
        out_specs=[const((nb * tm, D_MODEL))] + state_specs,
        out_shape=[
            jax.ShapeDtypeStruct((nb * tm, D_MODEL), F32),
            jax.ShapeDtypeStruct((depth, nb, N_HEADS, HEAD_DIM, HEAD_DIM), F32),
            jax.ShapeDtypeStruct((depth, nb, CONV_K - 1, CONV_WIDTH), F32),
        ],
        input_output_aliases=aliases,
        scratch_shapes=[
            pltpu.VMEM((nb * tm, D_MODEL), BF16),
            pltpu.VMEM((F32_SUBLANES + tm, CONV_WIDTH), F32),
        ],
        compiler_params=pltpu.CompilerParams(
            dimension_semantics=("arbitrary",),
            vmem_limit_bytes=V7X_VMEM_LIMIT_BYTES),
        name="sample_mixer",
    )(x2d, params["ln1"], params["w_in"], cos2, sin2, dmat, qdec, kdec, cdec, params["conv_w"],
      params["ret_w"], params["w_out"], r_all, c_all, *prev_states)


def _sample_mlp_kernel(x_ref, ln2_ref, wup_ref, wdown_ref, lnf_ref, o_ref, h2_ref, *, n_chunks, final_norm):
    c = pl.program_id(0)

    @pl.when(c == 0)
    def _():
        x = x_ref[...]
        h2_ref[...] = _rms_norm(x, ln2_ref[...]).astype(BF16)
        o_ref[...] = x

    up = jnp.dot(h2_ref[...], wup_ref[...], preferred_element_type=F32)
    act = jnp.square(jnp.maximum(up, 0.0)).astype(BF16)
    o_ref[...] += jnp.dot(act, wdown_ref[...], preferred_element_type=F32)

    if final_norm:
        @pl.when(c == n_chunks - 1)
        def _():
            o_ref[...] = _rms_norm(o_ref[...], lnf_ref[...])


def _sample_mlp_call(x2d, layer, params, *, final_norm):
    t, _ = x2d.shape
    n_chunks = D_FF // SAMPLE_FF_CHUNK
    const = lambda shape: pl.BlockSpec(shape, lambda c: (0,) * len(shape))
    return pl.pallas_call(
        functools.partial(_sample_mlp_kernel, n_chunks=n_chunks, final_norm=final_norm),
        grid=(n_chunks,),
        in_specs=[
            const((t, D_MODEL)),
            pl.BlockSpec((None, 1, D_MODEL), lambda c: (layer, 0, 0)),
            pl.BlockSpec((None, D_MODEL, SAMPLE_FF_CHUNK), lambda c: (layer, 0, c)),
            pl.BlockSpec((None, SAMPLE_FF_CHUNK, D_MODEL), lambda c: (layer, c, 0)),
            const((1, D_MODEL)),
        ],
        out_specs=const((t, D_MODEL)),
        out_shape=jax.ShapeDtypeStruct((t, D_MODEL), F32),
        scratch_shapes=[pltpu.VMEM((t, D_MODEL), BF16)],
        compiler_params=pltpu.CompilerParams(
            dimension_semantics=("arbitrary",),
            vmem_limit_bytes=V7X_VMEM_LIMIT_BYTES),
        name="sample_mlp",
    )(x2d, params["ln2"], params["w_up"], params["w_down"], params["ln_f"])


def kernel(x_prompt, x_sample, state_ret, state_conv, ln1_w, w_in, conv_w, ret_norm_w, w_out, ln2_w, w_mlp_up, w_mlp_down, ln_f_w):
    depth = w_in.shape[0]
    b_p, l_p, _ = x_prompt.shape
    b_s, l_s, _ = x_sample.shape

    def to_bf16(w):
        return w.reshape(-1, w.shape[-1]).astype(BF16).reshape(w.shape)

    params = dict(
        ln1=ln1_w.reshape(depth, 1, D_MODEL),
        w_in=to_bf16(w_in),
        conv_w=jnp.swapaxes(conv_w, 1, 2),
        ret_w=ret_norm_w.reshape(depth, 1, RET_WIDTH),
        w_out=to_bf16(w_out),
        ln2=ln2_w.reshape(depth, 1, D_MODEL),
        w_up=to_bf16(w_mlp_up),
        w_down=to_bf16(w_mlp_down),
        ln_f=ln_f_w.reshape(1, D_MODEL),
    )

    xp = x_prompt.reshape(b_p * l_p, D_MODEL)
    xs = x_sample.reshape(b_s * l_s, D_MODEL)
    states_p, states_s = (), ()
    for i in range(depth):
        last = i == depth - 1
        xp, *states_p = _prompt_layer_call(xp, l_p, i, depth, params, states_p, final_norm=last)
        xs, *states_s = _sample_mixer_call(xs, state_ret, state_conv, i, depth, params, states_s,
                                           nb=b_s, tm=l_s)
        xs = _sample_mlp_call(xs, i, params, final_norm=last)

    return (xp.reshape(b_p, l_p, D_MODEL), xs.reshape(b_s, l_s, D_MODEL),
            states_p[0], states_p[1], states_s[0], states_s[1])
```
